```python
import jax, jax.numpy as jnp
from jax import lax
import numpy as np

D_MODEL = 4096
BATCH = 1
SEQ = 16384
DEPTH = 1

MIX_WIDTH = D_MODEL
GMLP_WIDTH = MIX_WIDTH // 2
GMLP_HEADS = 8
GMLP_HEAD_DIM = GMLP_WIDTH // GMLP_HEADS
GMLP_CHUNK = 128
GLA_WIDTH = MIX_WIDTH - GMLP_WIDTH
GLA_HEADS = 4
GLA_KEY_WIDTH = GLA_WIDTH // 2
GLA_HEAD_DK = GLA_KEY_WIDTH // GLA_HEADS
GLA_HEAD_DV = GLA_WIDTH // GLA_HEADS
GLA_GATE_RANK = 16
GLA_GATE_NORMALIZER = 16.0
GLA_CHUNK = 64
N_GROUPS = 8
EXPERTS_PER_GROUP = 8
N_EXPERTS = N_GROUPS * EXPERTS_PER_GROUP
TOP_K_IN_GROUP = 2
D_EXPERT = D_MODEL // 4
MOE_BLOCK = 128
NORM_EPS = 1e-6
PROJ_SPLITS = (GMLP_WIDTH, GMLP_WIDTH, GLA_KEY_WIDTH, GLA_KEY_WIDTH, GLA_WIDTH, GLA_WIDTH, GLA_GATE_RANK)
D_IN_PROJ = GMLP_WIDTH * 2 + GLA_KEY_WIDTH * 2 + GLA_WIDTH * 2 + GLA_GATE_RANK

kernel_name = 'hymba_gmlp_gla_hier_moe'


def rmsnorm(x, g):
    xf = x.astype(jnp.float32)
    y = xf * lax.rsqrt(jnp.mean(xf * xf, axis=-1, keepdims=True) + NORM_EPS)
    return (y * g.astype(jnp.float32)).astype(x.dtype)


def layernorm(x, g):
    xf = x.astype(jnp.float32)
    xc = xf - jnp.mean(xf, axis=-1, keepdims=True)
    y = xc * lax.rsqrt(jnp.mean(xc * xc, axis=-1, keepdims=True) + NORM_EPS)
    return (y * g.astype(jnp.float32)).astype(x.dtype)


def gmlp_mixer(u, v, v_g, ws, bs):
    B, T, _ = v.shape
    nc = T // GMLP_CHUNK
    u = jax.nn.gelu(u, approximate=False)
    v = layernorm(jax.nn.gelu(v, approximate=False), v_g)
    vh = v.reshape(B, nc, GMLP_CHUNK, GMLP_HEADS, GMLP_HEAD_DIM)
    causal = jnp.tril(jnp.ones((GMLP_CHUNK, GMLP_CHUNK), dtype=bool))
    ws_masked = jnp.where(causal[None], ws, jnp.zeros((), ws.dtype))
    s = jnp.einsum('hij,bcjhd->bcihd', ws_masked, vh) + bs.T[None, None, :, :, None]
    return u * s.reshape(B, T, GMLP_WIDTH)


def gla_mixer(q, k, v, r, lr, wa2, ba, g_norm):
    B, T, _ = q.shape
    nc = T // GLA_CHUNK
    f32 = jnp.float32
    logit = lr.astype(f32) @ wa2.astype(f32) + ba.astype(f32)
    log_a = jax.nn.log_sigmoid(logit) / GLA_GATE_NORMALIZER
    shp_k = (B, nc, GLA_CHUNK, GLA_HEADS, GLA_HEAD_DK)
    qh = q.astype(f32).reshape(shp_k) * (GLA_HEAD_DK ** -0.5)
    kh = k.astype(f32).reshape(shp_k)
    vh = v.astype(f32).reshape(B, nc, GLA_CHUNK, GLA_HEADS, GLA_HEAD_DV)
    bcum = jnp.cumsum(log_a.reshape(shp_k), axis=2)
    b_last = bcum[:, :, -1:]
    q_e = qh * jnp.exp(bcum)
    k_e = kh * jnp.exp(-bcum)
    k_end = kh * jnp.exp(b_last - bcum)
    causal = jnp.tril(jnp.ones((GLA_CHUNK, GLA_CHUNK), dtype=bool))
    scores = jnp.einsum('bnihd,bnjhd->bnhij', q_e, k_e)
    scores = jnp.where(causal, scores, 0.0)
    o_intra = jnp.einsum('bnhij,bnjhv->bnihv', scores, vh)
    decay = jnp.exp(b_last[:, :, 0])

    def step(S, inp):
        q_c, k_c, v_c, d_c = inp
        o = jnp.einsum('bihd,bhdv->bihv', q_c, S)
        S = d_c[..., None] * S + jnp.einsum('bjhd,bjhv->bhdv', k_c, v_c)
        return S, o

    xs = (jnp.moveaxis(q_e, 1, 0), jnp.moveaxis(k_end, 1, 0), jnp.moveaxis(vh, 1, 0), jnp.moveaxis(decay, 1, 0))
    S0 = jnp.zeros((B, GLA_HEADS, GLA_HEAD_DK, GLA_HEAD_DV), f32)
    _, o_inter = lax.scan(step, S0, xs)
    o = o_intra + jnp.moveaxis(o_inter, 0, 1)
    o = o.reshape(B, T, GLA_HEADS, GLA_HEAD_DV)
    o = o * lax.rsqrt(jnp.mean(o * o, axis=-1, keepdims=True) + NORM_EPS) * g_norm.astype(f32)
    o = o.reshape(B, T, GLA_WIDTH) * jax.nn.silu(r.astype(f32))
    return o.astype(q.dtype)


def hierarchical_moe(h, wc, bc, wf, bf, w_gate, w_up, w_down):
    T = h.shape[0]
    f32 = jnp.float32
    hf = h.astype(f32)
    coarse = hf @ wc.astype(f32) + bc.astype(f32)
    p_group = jax.nn.softmax(coarse, axis=-1)
    g_sel = jnp.argmax(coarse, axis=-1)
    p_sel = jnp.take_along_axis(p_group, g_sel[:, None], axis=1)[:, 0]
    fine = (hf @ wf.astype(f32) + bf.astype(f32)).reshape(T, N_GROUPS, EXPERTS_PER_GROUP)
    fine_g = jnp.take_along_axis(fine, g_sel[:, None, None], axis=1)[:, 0]
    top_v, top_i = lax.top_k(fine_g, TOP_K_IN_GROUP)
    gate = jax.nn.softmax(top_v, axis=-1) * p_sel[:, None]
    expert = g_sel[:, None] * EXPERTS_PER_GROUP + top_i
    n_assign = T * TOP_K_IN_GROUP
    flat_e = expert.reshape(-1).astype(jnp.int32)
    flat_tok = jnp.repeat(jnp.arange(T, dtype=jnp.int32), TOP_K_IN_GROUP)
    flat_w = gate.reshape(-1)
    order = jnp.argsort(flat_e)
    se, stok, sw = flat_e[order], flat_tok[order], flat_w[order]
    counts = jax.ops.segment_sum(jnp.ones_like(flat_e), flat_e, num_segments=N_EXPERTS)
    starts = jnp.cumsum(counts) - counts
    padded = ((counts + MOE_BLOCK - 1) // MOE_BLOCK) * MOE_BLOCK
    pends = jnp.cumsum(padded)
    pstarts = pends - padded
    dest = pstarts[se] + (jnp.arange(n_assign, dtype=jnp.int32) - starts[se])
    n_pad = n_assign + N_EXPERTS * MOE_BLOCK
    n_blocks = n_pad // MOE_BLOCK
    tok_buf = jnp.zeros((n_pad,), jnp.int32).at[dest].set(stok)
    w_buf = jnp.zeros((n_pad,), f32).at[dest].set(sw)
    block_start = jnp.arange(n_blocks, dtype=jnp.int32) * MOE_BLOCK
    block_e = jnp.minimum(jnp.searchsorted(pends, block_start, side='right'), N_EXPERTS - 1)

    def expert_block(args):
        tok, wt, e = args
        xb = h[tok]
        a = jax.nn.silu(xb @ w_gate[e]) * (xb @ w_up[e])
        y = a @ w_down[e]
        return (y.astype(f32) * wt[:, None]).astype(h.dtype)

    y_blocks = lax.map(expert_block, (tok_buf.reshape(n_blocks, MOE_BLOCK), w_buf.reshape(n_blocks, MOE_BLOCK), block_e))
    return jnp.zeros_like(h).at[tok_buf].add(y_blocks.reshape(n_pad, h.shape[1]))


def setup_inputs(seed: int = 0) -> dict:
    key = jax.random.key(seed)
    ks = jax.random.split(key, 20)
    n = jax.random.normal
    L, D = DEPTH, D_MODEL
    return {
        'x': n(ks[0], (BATCH, SEQ, D), jnp.float32),
        'norm1_g': 1.0 + 0.02 * n(ks[1], (L, D), jnp.float32),
        'w_in': n(ks[2], (L, D, D_IN_PROJ), jnp.float32) * D ** -0.5,
        'gmlp_v_g': 1.0 + 0.02 * n(ks[3], (L, GMLP_WIDTH), jnp.float32),
        'gmlp_ws': n(ks[4], (L, GMLP_HEADS, GMLP_CHUNK, GMLP_CHUNK), jnp.float32) * GMLP_CHUNK ** -0.5,
        'gmlp_bs': 1.0 + 0.02 * n(ks[5], (L, GMLP_HEADS, GMLP_CHUNK), jnp.float32),
        'gla_wa2': n(ks[6], (L, GLA_GATE_RANK, GLA_KEY_WIDTH), jnp.float32) * GLA_GATE_RANK ** -0.5,
        'gla_ba': 0.1 * n(ks[7], (L, GLA_KEY_WIDTH), jnp.float32),
        'gla_norm_g': 1.0 + 0.02 * n(ks[8], (L, GLA_HEAD_DV), jnp.float32),
        'w_out': n(ks[9], (L, MIX_WIDTH, D), jnp.float32) * MIX_WIDTH ** -0.5,
        'norm2_g': 1.0 + 0.02 * n(ks[10], (L, D), jnp.float32),
        'router_coarse_w': n(ks[11], (L, D, N_GROUPS), jnp.float32) * D ** -0.5,
        'router_coarse_b': 0.01 * n(ks[12], (L, N_GROUPS), jnp.float32),
        'router_fine_w': n(ks[13], (L, D, N_EXPERTS), jnp.float32) * D ** -0.5,
        'router_fine_b': 0.01 * n(ks[14], (L, N_EXPERTS), jnp.float32),
        'exp_w_gate': n(ks[15], (L, N_EXPERTS, D, D_EXPERT), jnp.float32) * D ** -0.5,
        'exp_w_up': n(ks[16], (L, N_EXPERTS, D, D_EXPERT), jnp.float32) * D ** -0.5,
        'exp_w_down': n(ks[17], (L, N_EXPERTS, D_EXPERT, D), jnp.float32) * D_EXPERT ** -0.5,
        'norm_f_g': 1.0 + 0.02 * n(ks[18], (D,), jnp.float32),
    }


def reference(x, norm1_g, w_in, gmlp_v_g, gmlp_ws, gmlp_bs, gla_wa2, gla_ba, gla_norm_g, w_out, norm2_g,
              router_coarse_w, router_coarse_b, router_fine_w, router_fine_b, exp_w_gate, exp_w_up, exp_w_down, norm_f_g):
    B, T, D = x.shape
    offsets = [int(o) for o in np.cumsum(PROJ_SPLITS)[:-1]]
    h = x
    for l in range(DEPTH):
        xn = rmsnorm(h, norm1_g[l])
        z = xn @ w_in[l]
        u_a, v_a, q_b, k_b, v_b, r_b, lr_b = jnp.split(z, offsets, axis=-1)
        y_a = gmlp_mixer(u_a, v_a, gmlp_v_g[l], gmlp_ws[l], gmlp_bs[l])
        y_b = gla_mixer(q_b, k_b, v_b, r_b, lr_b, gla_wa2[l], gla_ba[l], gla_norm_g[l])
        mix = jnp.concatenate([y_a, y_b], axis=-1)
        h = h + mix @ w_out[l]
        hn = rmsnorm(h, norm2_g[l]).reshape(B * T, D)
        y = hierarchical_moe(hn, router_coarse_w[l], router_coarse_b[l], router_fine_w[l], router_fine_b[l],
                             exp_w_gate[l], exp_w_up[l], exp_w_down[l])
        h = h + y.reshape(B, T, D)
    return rmsnorm(h, norm_f_g)
```

```python
import functools

import jax
import jax.numpy as jnp
from jax import lax
from jax.experimental import pallas as pl
from jax.experimental.pallas import tpu as pltpu

F32 = jnp.float32
BF16 = jnp.bfloat16
U32 = jnp.uint32
I32 = jnp.int32

D_MODEL = 4096
GMLP_WIDTH = 2048
GMLP_HEADS = 8
GMLP_HEAD_DIM = GMLP_WIDTH // GMLP_HEADS
GMLP_CHUNK = 128
GLA_WIDTH = 2048
GLA_HEADS = 4
GLA_KEY_WIDTH = 1024
GLA_HEAD_DK = GLA_KEY_WIDTH // GLA_HEADS
GLA_HEAD_DV = GLA_WIDTH // GLA_HEADS
GLA_GATE_RANK = 16
GLA_GATE_NORMALIZER = 16.0
GLA_CHUNK = 64
N_GROUPS = 8
EXPERTS_PER_GROUP = 8
N_EXPERTS = N_GROUPS * EXPERTS_PER_GROUP
D_EXPERT = D_MODEL // 4
NORM_EPS = 1e-6
D_PROJ_MAIN = 2 * GMLP_WIDTH + 2 * GLA_KEY_WIDTH + 2 * GLA_WIDTH

LANES = 128
HALF = D_MODEL // 2
VMEM_LIMIT = 58 * 1024 * 1024

HIGHEST = lax.Precision.HIGHEST


def _params(semantics, vmem=VMEM_LIMIT):
    return pltpu.CompilerParams(dimension_semantics=semantics, vmem_limit_bytes=vmem)


def _inproj_body(x_ref, g_ref, w_ref, wlr_ref, z_ref, lr_ref, xn_ref):
    @pl.when(pl.program_id(1) == 0)
    def _():
        x = x_ref[...]
        ms = jnp.mean(x * x, axis=-1, keepdims=True)
        xn = ((x * lax.rsqrt(ms + NORM_EPS)) * g_ref[...]).astype(BF16)
        xn_ref[...] = xn
        lr_ref[...] = jnp.dot(xn, wlr_ref[...], preferred_element_type=F32)

    z_ref[...] = jnp.dot(xn_ref[...], w_ref[...], preferred_element_type=F32)


def _inproj(x, g, w_main, w_lr, *, tm, tn):
    t = x.shape[0]
    return pl.pallas_call(
        _inproj_body,
        grid=(t // tm, D_PROJ_MAIN // tn),
        in_specs=[
            pl.BlockSpec((tm, D_MODEL), lambda i, j: (i, 0)),
            pl.BlockSpec((1, D_MODEL), lambda i, j: (0, 0)),
            pl.BlockSpec((D_MODEL, tn), lambda i, j: (0, j)),
            pl.BlockSpec((D_MODEL, LANES), lambda i, j: (0, 0)),
        ],
        out_specs=[
            pl.BlockSpec((tm, tn), lambda i, j: (i, j)),
            pl.BlockSpec((tm, LANES), lambda i, j: (i, 0)),
        ],
        out_shape=[
            jax.ShapeDtypeStruct((t, D_PROJ_MAIN), F32),
            jax.ShapeDtypeStruct((t, LANES), F32),
        ],
        scratch_shapes=[pltpu.VMEM((tm, D_MODEL), BF16)],
        compiler_params=_params(("arbitrary", "arbitrary")),
        name="inproj",
    )(x, g, w_main, w_lr)


def _gelu(x):
    return x * (lax.erf(x * (2.0 ** -0.5)) + 1.0) * 0.5


def _gmlp_body(u_ref, v_ref, vg_ref, ws_ref, bst_ref, y_ref, *, n_chunks):
    row = lax.broadcasted_iota(I32, (GMLP_CHUNK, GMLP_CHUNK), 0)
    col = lax.broadcasted_iota(I32, (GMLP_CHUNK, GMLP_CHUNK), 1)
    causal = row >= col
    for c in range(n_chunks):
        rows = slice(c * GMLP_CHUNK, (c + 1) * GMLP_CHUNK)
        gv = _gelu(v_ref[rows, :])
        mu = jnp.mean(gv, axis=-1, keepdims=True)
        vc = gv - mu
        var = jnp.mean(vc * vc, axis=-1, keepdims=True)
        vn = ((vc * lax.rsqrt(var + NORM_EPS)) * vg_ref[...]).astype(BF16)
        for h in range(GMLP_HEADS):
            cols = slice(h * GMLP_HEAD_DIM, (h + 1) * GMLP_HEAD_DIM)
            wm = jnp.where(causal, ws_ref[h], 0.0).astype(BF16)
            s = jnp.dot(wm, vn[:, cols], preferred_element_type=F32) + bst_ref[:, h:h + 1]
            y_ref[rows, cols] = (_gelu(u_ref[rows, cols]) * s).astype(BF16)


def _gmlp(z, v_g, ws, bs_t, *, tb):
    t = z.shape[0]
    blk = GMLP_WIDTH
    return pl.pallas_call(
        functools.partial(_gmlp_body, n_chunks=tb // GMLP_CHUNK),
        grid=(t // tb,),
        in_specs=[
            pl.BlockSpec((tb, blk), lambda i: (i, 0)),
            pl.BlockSpec((tb, blk), lambda i: (i, 1)),
            pl.BlockSpec((1, blk), lambda i: (0, 0)),
            pl.BlockSpec((GMLP_HEADS, GMLP_CHUNK, GMLP_CHUNK), lambda i: (0, 0, 0)),
            pl.BlockSpec((GMLP_CHUNK, GMLP_HEADS), lambda i: (0, 0)),
        ],
        out_specs=pl.BlockSpec((tb, blk), lambda i: (i, 0)),
        out_shape=jax.ShapeDtypeStruct((t, blk), BF16),
        compiler_params=_params(("arbitrary",)),
        name="gmlp",
    )(z, z, v_g, ws, bs_t)


def _gla_body(q_ref, k_ref, v_ref, r_ref, lr_ref, wa2_ref, ba_ref, gn_ref, y_ref, s_ref, *, n_chunks):
    @pl.when(pl.program_id(0) == 0)
    def _():
        s_ref[...] = jnp.zeros_like(s_ref)

    c_len = GLA_CHUNK
    row = lax.broadcasted_iota(I32, (c_len, c_len), 0)
    col = lax.broadcasted_iota(I32, (c_len, c_len), 1)
    causal = row >= col
    ltri = jnp.where(causal, 1.0, 0.0).astype(F32)
    ones_c = jnp.ones((c_len, LANES), F32)
    tn_dims = (((0,), (0,)), ((), ()))
    nt_dims = (((1,), (1,)), ((), ()))

    def chunk(c, carry):
        t0 = pl.multiple_of(c * c_len, c_len)
        rows = pl.ds(t0, c_len)
        lr = lr_ref[rows, :]
        for h in range(GLA_HEADS):
            kc = slice(h * GLA_HEAD_DK, (h + 1) * GLA_HEAD_DK)
            vc = slice(h * GLA_HEAD_DV, (h + 1) * GLA_HEAD_DV)
            logit = jnp.dot(lr, wa2_ref[:, kc], precision=HIGHEST, preferred_element_type=F32) + ba_ref[:, kc]
            log_a = (jnp.minimum(logit, 0.0) - jnp.log1p(jnp.exp(-jnp.abs(logit)))) * (1.0 / GLA_GATE_NORMALIZER)
            bcum = jnp.dot(ltri, log_a, precision=HIGHEST, preferred_element_type=F32)
            b_last = bcum[c_len - 1:c_len, :]
            q = q_ref[rows, kc] * (GLA_HEAD_DK ** -0.5)
            k = k_ref[rows, kc]
            q_e = (q * jnp.exp(bcum)).astype(BF16)
            k_e = (k * jnp.exp(-bcum)).astype(BF16)
            k_end = (k * jnp.exp(b_last - bcum)).astype(BF16)
            v = v_ref[rows, vc].astype(BF16)
            scores = lax.dot_general(q_e, k_e, nt_dims, preferred_element_type=F32)
            scores = jnp.where(causal, scores, 0.0).astype(BF16)
            state = s_ref[h]
            o = jnp.dot(scores, v, preferred_element_type=F32)
            o = o + jnp.dot(q_e, state.astype(BF16), preferred_element_type=F32)
            dcol = lax.dot_general(log_a, ones_c, tn_dims, precision=HIGHEST, preferred_element_type=F32)
            decay = jnp.exp(dcol)
            decay = jnp.concatenate([decay] * (GLA_HEAD_DV // LANES), axis=1)
            kv = lax.dot_general(k_end, v, tn_dims, preferred_element_type=F32)
            s_ref[h] = decay * state + kv
            o = (o * lax.rsqrt(jnp.mean(o * o, axis=-1, keepdims=True) + NORM_EPS)) * gn_ref[...]
            r = r_ref[rows, vc]
            y_ref[rows, vc] = (o * (r * jax.nn.sigmoid(r))).astype(BF16)
        return carry

    lax.fori_loop(0, n_chunks, chunk, 0)


def _gla(z, lr, wa2p, ba, gn, *, tb):
    t = z.shape[0]
    kw, vw = GLA_KEY_WIDTH, GLA_WIDTH
    return pl.pallas_call(
        functools.partial(_gla_body, n_chunks=tb // GLA_CHUNK),
        grid=(t // tb,),
        in_specs=[
            pl.BlockSpec((tb, kw), lambda i: (i, 4)),
            pl.BlockSpec((tb, kw), lambda i: (i, 5)),
            pl.BlockSpec((tb, vw), lambda i: (i, 3)),
            pl.BlockSpec((tb, vw), lambda i: (i, 4)),
            pl.BlockSpec((tb, LANES), lambda i: (i, 0)),
            pl.BlockSpec((LANES, kw), lambda i: (0, 0)),
            pl.BlockSpec((1, kw), lambda i: (0, 0)),
            pl.BlockSpec((1, GLA_HEAD_DV), lambda i: (0, 0)),
        ],
        out_specs=pl.BlockSpec((tb, vw), lambda i: (i, 0)),
        out_shape=jax.ShapeDtypeStruct((t, vw), BF16),
        scratch_shapes=[pltpu.VMEM((GLA_HEADS, GLA_HEAD_DK, GLA_HEAD_DV), F32)],
        compiler_params=_params(("arbitrary",)),
        name="gla",
    )(z, z, z, z, lr, wa2p, ba, gn)


def _bf16_bits(x):
    return lax.bitcast_convert_type(x.astype(BF16).astype(F32), U32)


def _outproj_body(ya_ref, yb_ref, x_ref, woa_ref, wob_ref, g2_ref, wr_ref, br_ref,
                  h1_ref, hp_ref, gate_ref, eid_ref, hacc_ref, *, nj, tn):
    j = pl.program_id(1)
    acc = jnp.dot(ya_ref[...], woa_ref[...], preferred_element_type=F32)
    acc = acc + jnp.dot(yb_ref[...], wob_ref[...], preferred_element_type=F32)
    h = x_ref[...] + acc
    h1_ref[...] = h
    hacc_ref[j] = h

    @pl.when(j == nj - 1)
    def _():
        ssq = jnp.sum(hacc_ref[0] * hacc_ref[0], axis=-1, keepdims=True)
        for jj in range(1, nj):
            ssq = ssq + jnp.sum(hacc_ref[jj] * hacc_ref[jj], axis=-1, keepdims=True)
        rstd = lax.rsqrt(ssq * (1.0 / D_MODEL) + NORM_EPS)
        logits = br_ref[...]
        half_blocks = nj // 2
        for jj in range(half_blocks):
            lo = (hacc_ref[jj] * rstd) * g2_ref[:, jj * tn:(jj + 1) * tn]
            hi = (hacc_ref[jj + half_blocks] * rstd) * g2_ref[:, HALF + jj * tn:HALF + (jj + 1) * tn]
            logits = logits + jnp.dot(lo, wr_ref[jj * tn:(jj + 1) * tn, :],
                                      precision=HIGHEST, preferred_element_type=F32)
            logits = logits + jnp.dot(hi, wr_ref[HALF + jj * tn:HALF + (jj + 1) * tn, :],
                                      precision=HIGHEST, preferred_element_type=F32)
            hp_ref[:, jj * tn:(jj + 1) * tn] = (_bf16_bits(hi) & jnp.uint32(0xFFFF0000)) | (_bf16_bits(lo) >> 16)

        lane = lax.broadcasted_iota(I32, logits.shape, 1)
        neg = jnp.float32(-jnp.inf)
        big = jnp.int32(LANES)
        cmask = lane < N_GROUPS
        cl = jnp.where(cmask, logits, neg)
        cmax = jnp.max(cl, axis=-1, keepdims=True)
        g_sel = jnp.min(jnp.where(cl == cmax, lane, big), axis=-1, keepdims=True)
        p_sel = 1.0 / jnp.sum(jnp.where(cmask, jnp.exp(logits - cmax), 0.0), axis=-1, keepdims=True)
        f0 = N_GROUPS + g_sel * EXPERTS_PER_GROUP
        fl = jnp.where((lane >= f0) & (lane < f0 + EXPERTS_PER_GROUP), logits, neg)
        v1 = jnp.max(fl, axis=-1, keepdims=True)
        i1 = jnp.min(jnp.where(fl == v1, lane, big), axis=-1, keepdims=True)
        fl2 = jnp.where(lane == i1, neg, fl)
        v2 = jnp.max(fl2, axis=-1, keepdims=True)
        i2 = jnp.min(jnp.where(fl2 == v2, lane, big), axis=-1, keepdims=True)
        e2 = jnp.exp(v2 - v1)
        w1 = p_sel / (1.0 + e2)
        w2 = p_sel * e2 / (1.0 + e2)
        gate_ref[...] = jnp.where(lane == 0, w1, jnp.where(lane == 1, w2, 0.0))
        eid_ref[...] = jnp.where(lane == 0, i1 - N_GROUPS, jnp.where(lane == 1, i2 - N_GROUPS, 0))


def _outproj(ya, yb, x, wo_a, wo_b, g2, wr, br, *, tm, tn):
    t = x.shape[0]
    nj = D_MODEL // tn
    return pl.pallas_call(
        functools.partial(_outproj_body, nj=nj, tn=tn),
        grid=(t // tm, nj),
        in_specs=[
            pl.BlockSpec((tm, GMLP_WIDTH), lambda i, j: (i, 0)),
            pl.BlockSpec((tm, GLA_WIDTH), lambda i, j: (i, 0)),
            pl.BlockSpec((tm, tn), lambda i, j: (i, j)),
            pl.BlockSpec((GMLP_WIDTH, tn), lambda i, j: (0, j)),
            pl.BlockSpec((GLA_WIDTH, tn), lambda i, j: (0, j)),
            pl.BlockSpec((1, D_MODEL), lambda i, j: (0, 0)),
            pl.BlockSpec((D_MODEL, LANES), lambda i, j: (0, 0)),
            pl.BlockSpec((1, LANES), lambda i, j: (0, 0)),
        ],
        out_specs=[
            pl.BlockSpec((tm, tn), lambda i, j: (i, j)),
            pl.BlockSpec((tm, HALF), lambda i, j: (i, 0)),
            pl.BlockSpec((tm, LANES), lambda i, j: (i, 0)),
            pl.BlockSpec((tm, LANES), lambda i, j: (i, 0)),
        ],
        out_shape=[
            jax.ShapeDtypeStruct((t, D_MODEL), F32),
            jax.ShapeDtypeStruct((t, HALF), U32),
            jax.ShapeDtypeStruct((t, LANES), F32),
            jax.ShapeDtypeStruct((t, LANES), I32),
        ],
        scratch_shapes=[pltpu.VMEM((nj, tm, tn), F32)],
        compiler_params=_params(("arbitrary", "arbitrary")),
        name="outproj_router",
    )(ya, yb, x, wo_a, wo_b, g2, wr, br)


def _gather_body(nsb_ref, tok_ref, hp_hbm, xs_ref, sem, *, rows):
    s = pl.program_id(0)

    def row_copy(i):
        return pltpu.make_async_copy(hp_hbm.at[pl.ds(tok_ref[0, 0, i], 1)], xs_ref.at[pl.ds(i, 1)], sem)

    @pl.when(s < nsb_ref[0])
    def _():
        def start(i, c):
            row_copy(i).start()
            return c

        def wait(i, c):
            row_copy(i).wait()
            return c

        lax.fori_loop(0, rows, start, 0)
        lax.fori_loop(0, rows, wait, 0)

    @pl.when(s >= nsb_ref[0])
    def _():
        xs_ref[...] = jnp.zeros_like(xs_ref)


def _gather(nsb, tok3, hp, *, rows):
    s_max = tok3.shape[0]
    return pl.pallas_call(
        functools.partial(_gather_body, rows=rows),
        grid_spec=pltpu.PrefetchScalarGridSpec(
            num_scalar_prefetch=1,
            grid=(s_max,),
            in_specs=[
                pl.BlockSpec((1, 1, rows), lambda s, nsb: (s, 0, 0), memory_space=pltpu.SMEM),
                pl.BlockSpec(memory_space=pl.ANY),
            ],
            out_specs=pl.BlockSpec((rows, HALF), lambda s, nsb: (s, 0)),
            scratch_shapes=[pltpu.SemaphoreType.DMA(())],
        ),
        out_shape=jax.ShapeDtypeStruct((s_max * rows, HALF), U32),
        compiler_params=_params(("arbitrary",)),
        name="moe_gather",
    )(nsb, tok3, hp)


FFN_PHASES = 4
FFN_SUB = 128


def _ffn_body(sbe_ref, sbr_ref, nsb_ref, xs_ref, wg_ref, wu_ref, wd_ref, ys_ref,
              xb_ref, a_ref, wgb_ref, wub_ref, wdb_ref, *, rows):
    s = pl.program_id(0)
    p = pl.program_id(1)
    valid = s < nsb_ref[0]
    nrows = sbr_ref[s]
    n_sub = rows // FFN_SUB

    @pl.when(valid & (p == 0))
    def _():
        w = xs_ref[...]
        xb_ref[:, :HALF] = lax.bitcast_convert_type(w << 16, F32).astype(BF16)
        xb_ref[:, HALF:] = lax.bitcast_convert_type(w & jnp.uint32(0xFFFF0000), F32).astype(BF16)

    @pl.when(valid & (p < FFN_PHASES))
    def _():
        wgb_ref[...] = wg_ref[...].astype(BF16)
        wub_ref[...] = wu_ref[...].astype(BF16)
        for rb in range(n_sub):
            @pl.when(rb * FFN_SUB < nrows)
            def _():
                r = slice(rb * FFN_SUB, (rb + 1) * FFN_SUB)
                xb = xb_ref[r, :]
                g = jnp.dot(xb, wgb_ref[...], preferred_element_type=F32)
                u = jnp.dot(xb, wub_ref[...], preferred_element_type=F32)
                a_ref[p, r, :] = ((g * jax.nn.sigmoid(g)) * u).astype(BF16)

    @pl.when(p >= FFN_PHASES)
    def _():
        @pl.when(valid)
        def _():
            wdb_ref[...] = wd_ref[...].astype(BF16)

        for rb in range(n_sub):
            r = slice(rb * FFN_SUB, (rb + 1) * FFN_SUB)

            @pl.when(rb * FFN_SUB < nrows)
            def _():
                a = jnp.concatenate([a_ref[c, r, :] for c in range(FFN_PHASES)], axis=1)
                ys_ref[r, :] = jnp.dot(a, wdb_ref[...], preferred_element_type=F32)

            @pl.when(rb * FFN_SUB >= nrows)
            def _():
                ys_ref[r, :] = jnp.zeros((FFN_SUB, ys_ref.shape[1]), F32)


def _ffn(sb_e, sb_rows, nsb, xs, w_gate, w_up, w_down, *, rows):
    s_max = sb_e.shape[0]
    fc = D_EXPERT // FFN_PHASES
    oc = D_MODEL // FFN_PHASES
    last = FFN_PHASES - 1

    def clamp_s(s, nsb):
        return jnp.minimum(s, nsb[0] - 1)

    def up_idx(s, p, sbe, sbr, nsb):
        return (sbe[clamp_s(s, nsb)], 0, jnp.where(s < nsb[0], jnp.minimum(p, last), last))

    def down_col(s, p, nsb):
        return jnp.where(s < nsb[0], jnp.maximum(p - FFN_PHASES, 0), last)

    return pl.pallas_call(
        functools.partial(_ffn_body, rows=rows),
        grid_spec=pltpu.PrefetchScalarGridSpec(
            num_scalar_prefetch=3,
            grid=(s_max, 2 * FFN_PHASES),
            in_specs=[
                pl.BlockSpec((rows, HALF), lambda s, p, sbe, sbr, nsb: (clamp_s(s, nsb), 0)),
                pl.BlockSpec((None, D_MODEL, fc), up_idx),
                pl.BlockSpec((None, D_MODEL, fc), up_idx),
                pl.BlockSpec((None, D_EXPERT, oc),
                             lambda s, p, sbe, sbr, nsb: (sbe[clamp_s(s, nsb)], 0, down_col(s, p, nsb))),
            ],
            out_specs=pl.BlockSpec((rows, oc), lambda s, p, sbe, sbr, nsb: (s, jnp.maximum(p - FFN_PHASES, 0))),
            scratch_shapes=[
                pltpu.VMEM((rows, D_MODEL), BF16),
                pltpu.VMEM((FFN_PHASES, rows, fc), BF16),
                pltpu.VMEM((D_MODEL, fc), BF16),
                pltpu.VMEM((D_MODEL, fc), BF16),
                pltpu.VMEM((D_EXPERT, oc), BF16),
            ],
        ),
        out_shape=jax.ShapeDtypeStruct((s_max * rows, D_MODEL), F32),
        compiler_params=_params(("arbitrary", "arbitrary")),
        name="moe_ffn",
    )(sb_e, sb_rows, nsb, xs, w_gate, w_up, w_down)


def _combine_body(pos_ref, h1_ref, gate_ref, gf_ref, ys_hbm, out_ref, buf_ref, sem, *, tb):
    def row_copy(k, i):
        return pltpu.make_async_copy(ys_hbm.at[pl.ds(pos_ref[0, k, i], 1)], buf_ref.at[k, pl.ds(i, 1)], sem)

    def start(i, c):
        row_copy(0, i).start()
        row_copy(1, i).start()
        return c

    def wait(i, c):
        row_copy(0, i).wait()
        row_copy(1, i).wait()
        return c

    lax.fori_loop(0, tb, start, 0)
    lax.fori_loop(0, tb, wait, 0)
    gate = gate_ref[...]
    y = buf_ref[0] * gate[:, 0:1] + buf_ref[1] * gate[:, 1:2]
    h = h1_ref[...] + y
    ms = jnp.mean(h * h, axis=-1, keepdims=True)
    out_ref[...] = (h * lax.rsqrt(ms + NORM_EPS)) * gf_ref[...]


def _combine(pos3, h1, gate, gf, ys, *, tb):
    t = h1.shape[0]
    return pl.pallas_call(
        functools.partial(_combine_body, tb=tb),
        grid=(t // tb,),
        in_specs=[
            pl.BlockSpec((1, 2, tb), lambda i: (i, 0, 0), memory_space=pltpu.SMEM),
            pl.BlockSpec((tb, D_MODEL), lambda i: (i, 0)),
            pl.BlockSpec((tb, LANES), lambda i: (i, 0)),
            pl.BlockSpec((1, D_MODEL), lambda i: (0, 0)),
            pl.BlockSpec(memory_space=pl.ANY),
        ],
        out_specs=pl.BlockSpec((tb, D_MODEL), lambda i: (i, 0)),
        out_shape=jax.ShapeDtypeStruct((t, D_MODEL), F32),
        scratch_shapes=[pltpu.VMEM((2, tb, D_MODEL), F32), pltpu.SemaphoreType.DMA(())],
        compiler_params=_params(("arbitrary",)),
        name="moe_combine",
    )(pos3, h1, gate, gf, ys)


def _dispatch_plan(eid, *, rows, s_max):
    t = eid.shape[0]
    flat_e = eid[:, :2].reshape(-1)
    onehot = (flat_e[:, None] == jnp.arange(N_EXPERTS, dtype=I32)[None, :]).astype(I32)
    csum = jnp.cumsum(onehot, axis=0)
    rank = jnp.sum(csum * onehot, axis=1) - 1
    counts = csum[-1]
    nsb_e = (counts + rows - 1) // rows
    sb_end = jnp.cumsum(nsb_e)
    sb_start = sb_end - nsb_e
    nsb = sb_end[-1]
    pos = sb_start[flat_e] * rows + rank
    s_idx = jnp.minimum(jnp.arange(s_max, dtype=I32), nsb - 1)
    sb_e = jnp.minimum(jnp.searchsorted(sb_end, s_idx, side="right"), N_EXPERTS - 1).astype(I32)
    sb_rows = jnp.clip(counts[sb_e] - (s_idx - sb_start[sb_e]) * rows, 0, rows)
    sb_rows = jnp.where(jnp.arange(s_max) < nsb, sb_rows, 0).astype(I32)
    flat_tok = jnp.arange(2 * t, dtype=I32) // 2
    tok = jnp.zeros((s_max * rows,), I32).at[pos].set(flat_tok)
    return sb_e, sb_rows, nsb.reshape(1).astype(I32), tok.reshape(s_max, 1, rows), pos.reshape(t, 2)


def _layer(x, norm1_g, w_in, gmlp_v_g, gmlp_ws, gmlp_bs, gla_wa2, gla_ba, gla_norm_g, w_out, norm2_g,
           router_coarse_w, router_coarse_b, router_fine_w, router_fine_b, exp_w_gate, exp_w_up, exp_w_down,
           norm_f_g, *, tm_in, tb_gmlp, tb_gla, tm_out, tb_comb, sb_rows):
    t = x.shape[0]
    w_main = w_in[:, :D_PROJ_MAIN].astype(BF16)
    w_lr = jnp.pad(w_in[:, D_PROJ_MAIN:], ((0, 0), (0, LANES - GLA_GATE_RANK))).astype(BF16)
    wa2p = jnp.pad(gla_wa2, ((0, LANES - GLA_GATE_RANK), (0, 0)))
    wo = w_out.astype(BF16)
    n_route = N_GROUPS + N_EXPERTS
    wr = jnp.pad(jnp.concatenate([router_coarse_w, router_fine_w], axis=1), ((0, 0), (0, LANES - n_route)))
    br = jnp.pad(jnp.concatenate([router_coarse_b, router_fine_b]), (0, LANES - n_route)).reshape(1, LANES)

    z, lr = _inproj(x, norm1_g.reshape(1, -1), w_main, w_lr, tm=tm_in, tn=1024)
    y_a = _gmlp(z, gmlp_v_g.reshape(1, -1), gmlp_ws, gmlp_bs.T, tb=tb_gmlp)
    y_b = _gla(z, lr, wa2p, gla_ba.reshape(1, -1), gla_norm_g.reshape(1, -1), tb=tb_gla)
    h1, hp, gate, eid = _outproj(y_a, y_b, x, wo[:GMLP_WIDTH], wo[GMLP_WIDTH:], norm2_g.reshape(1, -1),
                                 wr, br, tm=tm_out, tn=512)

    s_max = N_EXPERTS + (2 * t) // sb_rows
    sb_e, sb_n, nsb, tok3, pos = _dispatch_plan(eid, rows=sb_rows, s_max=s_max)
    xs = _gather(nsb, tok3, hp, rows=sb_rows)
    ys = _ffn(sb_e, sb_n, nsb, xs, exp_w_gate, exp_w_up, exp_w_down, rows=sb_rows)
    pos3 = pos.reshape(t // tb_comb, tb_comb, 2).transpose(0, 2, 1)
    return _combine(pos3, h1, gate, norm_f_g.reshape(1, -1), ys, tb=tb_comb)


def kernel(x, norm1_g, w_in, gmlp_v_g, gmlp_ws, gmlp_bs, gla_wa2, gla_ba, gla_norm_g, w_out, norm2_g,
           router_coarse_w, router_coarse_b, router_fine_w, router_fine_b, exp_w_gate, exp_w_up, exp_w_down,
           norm_f_g):
    b, t, d = x.shape
    assert b == 1 and d == D_MODEL and norm1_g.shape[0] == 1, "one sequence, one layer, the stated widths"

    def first(a):
        return a.reshape(a.shape[1:])

    out = _layer(
        first(x), first(norm1_g), first(w_in), first(gmlp_v_g), first(gmlp_ws), first(gmlp_bs), first(gla_wa2),
        first(gla_ba), first(gla_norm_g), first(w_out), first(norm2_g), first(router_coarse_w),
        first(router_coarse_b), first(router_fine_w), first(router_fine_b), first(exp_w_gate), first(exp_w_up),
        first(exp_w_down), norm_f_g,
        tm_in=512, tb_gmlp=256, tb_gla=512, tm_out=512, tb_comb=256, sb_rows=640)
    return out.reshape(b, t, d)
```

```python
import functools

import jax
import jax.numpy as jnp
from jax import lax
from jax.experimental import pallas as pl
from jax.experimental.pallas import tpu as pltpu

F32 = jnp.float32
BF16 = jnp.bfloat16
U32 = jnp.uint32
I32 = jnp.int32

D_MODEL = 4096
GMLP_WIDTH = 2048
GMLP_HEADS = 8
GMLP_HEAD_DIM = GMLP_WIDTH // GMLP_HEADS
GMLP_CHUNK = 128
GLA_WIDTH = 2048
GLA_HEADS = 4
GLA_KEY_WIDTH = 1024
GLA_HEAD_DK = GLA_KEY_WIDTH // GLA_HEADS
GLA_HEAD_DV = GLA_WIDTH // GLA_HEADS
GLA_GATE_RANK = 16
GLA_GATE_NORMALIZER = 16.0
GLA_CHUNK = 64
N_GROUPS = 8
EXPERTS_PER_GROUP = 8
N_EXPERTS = N_GROUPS * EXPERTS_PER_GROUP
D_EXPERT = D_MODEL // 4
NORM_EPS = 1e-6
D_PROJ_MAIN = 2 * GMLP_WIDTH + 2 * GLA_KEY_WIDTH + 2 * GLA_WIDTH

LANES = 128
HALF = D_MODEL // 2
ROW_TILES = HALF // LANES
VMEM_LIMIT = 58 * 1024 * 1024

HIGHEST = lax.Precision.HIGHEST


def _params(semantics, vmem=VMEM_LIMIT):
    return pltpu.CompilerParams(dimension_semantics=semantics, vmem_limit_bytes=vmem)


def _inproj_body(x_ref, g_ref, w_ref, wlr_ref, z_ref, lr_ref, xn_ref):
    @pl.when(pl.program_id(1) == 0)
    def _():
        x = x_ref[...]
        ms = jnp.mean(x * x, axis=-1, keepdims=True)
        xn = ((x * lax.rsqrt(ms + NORM_EPS)) * g_ref[...]).astype(BF16)
        xn_ref[...] = xn
        lr_ref[...] = jnp.dot(xn, wlr_ref[...], preferred_element_type=F32)

    z_ref[...] = jnp.dot(xn_ref[...], w_ref[...], preferred_element_type=F32)


def _inproj(x, g, w_main, w_lr, *, tm, tn):
    t = x.shape[0]
    return pl.pallas_call(
        _inproj_body,
        grid=(t // tm, D_PROJ_MAIN // tn),
        in_specs=[
            pl.BlockSpec((tm, D_MODEL), lambda i, j: (i, 0)),
            pl.BlockSpec((1, D_MODEL), lambda i, j: (0, 0)),
            pl.BlockSpec((D_MODEL, tn), lambda i, j: (0, j)),
            pl.BlockSpec((D_MODEL, LANES), lambda i, j: (0, 0)),
        ],
        out_specs=[
            pl.BlockSpec((tm, tn), lambda i, j: (i, j)),
            pl.BlockSpec((tm, LANES), lambda i, j: (i, 0)),
        ],
        out_shape=[
            jax.ShapeDtypeStruct((t, D_PROJ_MAIN), F32),
            jax.ShapeDtypeStruct((t, LANES), F32),
        ],
        scratch_shapes=[pltpu.VMEM((tm, D_MODEL), BF16)],
        compiler_params=_params(("arbitrary", "arbitrary")),
        name="inproj",
    )(x, g, w_main, w_lr)


def _gelu(x):
    return x * (lax.erf(x * (2.0 ** -0.5)) + 1.0) * 0.5


def _gmlp_body(u_ref, v_ref, vg_ref, ws_ref, bst_ref, y_ref, *, n_chunks):
    row = lax.broadcasted_iota(I32, (GMLP_CHUNK, GMLP_CHUNK), 0)
    col = lax.broadcasted_iota(I32, (GMLP_CHUNK, GMLP_CHUNK), 1)
    causal = row >= col
    for c in range(n_chunks):
        rows = slice(c * GMLP_CHUNK, (c + 1) * GMLP_CHUNK)
        gv = _gelu(v_ref[rows, :])
        mu = jnp.mean(gv, axis=-1, keepdims=True)
        vc = gv - mu
        var = jnp.mean(vc * vc, axis=-1, keepdims=True)
        vn = ((vc * lax.rsqrt(var + NORM_EPS)) * vg_ref[...]).astype(BF16)
        for h in range(GMLP_HEADS):
            cols = slice(h * GMLP_HEAD_DIM, (h + 1) * GMLP_HEAD_DIM)
            wm = jnp.where(causal, ws_ref[h], 0.0).astype(BF16)
            s = jnp.dot(wm, vn[:, cols], preferred_element_type=F32) + bst_ref[:, h:h + 1]
            y_ref[rows, cols] = (_gelu(u_ref[rows, cols]) * s).astype(BF16)


def _gmlp(z, v_g, ws, bs_t, *, tb):
    t = z.shape[0]
    blk = GMLP_WIDTH
    return pl.pallas_call(
        functools.partial(_gmlp_body, n_chunks=tb // GMLP_CHUNK),
        grid=(t // tb,),
        in_specs=[
            pl.BlockSpec((tb, blk), lambda i: (i, 0)),
            pl.BlockSpec((tb, blk), lambda i: (i, 1)),
            pl.BlockSpec((1, blk), lambda i: (0, 0)),
            pl.BlockSpec((GMLP_HEADS, GMLP_CHUNK, GMLP_CHUNK), lambda i: (0, 0, 0)),
            pl.BlockSpec((GMLP_CHUNK, GMLP_HEADS), lambda i: (0, 0)),
        ],
        out_specs=pl.BlockSpec((tb, blk), lambda i: (i, 0)),
        out_shape=jax.ShapeDtypeStruct((t, blk), BF16),
        compiler_params=_params(("arbitrary",)),
        name="gmlp",
    )(z, z, v_g, ws, bs_t)


def _gla_body(q_ref, k_ref, v_ref, r_ref, lr_ref, wa2_ref, ba_ref, gn_ref, y_ref, s_ref, *, n_chunks):
    @pl.when(pl.program_id(0) == 0)
    def _():
        s_ref[...] = jnp.zeros_like(s_ref)

    c_len = GLA_CHUNK
    row = lax.broadcasted_iota(I32, (c_len, c_len), 0)
    col = lax.broadcasted_iota(I32, (c_len, c_len), 1)
    causal = row >= col
    ltri = jnp.where(causal, 1.0, 0.0).astype(F32)
    ones_c = jnp.ones((c_len, LANES), F32)
    tn_dims = (((0,), (0,)), ((), ()))
    nt_dims = (((1,), (1,)), ((), ()))

    def chunk(c, carry):
        t0 = pl.multiple_of(c * c_len, c_len)
        rows = pl.ds(t0, c_len)
        lr = lr_ref[rows, :]
        for h in range(GLA_HEADS):
            kc = slice(h * GLA_HEAD_DK, (h + 1) * GLA_HEAD_DK)
            vc = slice(h * GLA_HEAD_DV, (h + 1) * GLA_HEAD_DV)
            logit = jnp.dot(lr, wa2_ref[:, kc], precision=HIGHEST, preferred_element_type=F32) + ba_ref[:, kc]
            log_a = (jnp.minimum(logit, 0.0) - jnp.log1p(jnp.exp(-jnp.abs(logit)))) * (1.0 / GLA_GATE_NORMALIZER)
            bcum = jnp.dot(ltri, log_a, precision=HIGHEST, preferred_element_type=F32)
            b_last = bcum[c_len - 1:c_len, :]
            q = q_ref[rows, kc] * (GLA_HEAD_DK ** -0.5)
            k = k_ref[rows, kc]
            q_e = (q * jnp.exp(bcum)).astype(BF16)
            k_e = (k * jnp.exp(-bcum)).astype(BF16)
            k_end = (k * jnp.exp(b_last - bcum)).astype(BF16)
            v = v_ref[rows, vc].astype(BF16)
            scores = lax.dot_general(q_e, k_e, nt_dims, preferred_element_type=F32)
            scores = jnp.where(causal, scores, 0.0).astype(BF16)
            state = s_ref[h]
            o = jnp.dot(scores, v, preferred_element_type=F32)
            o = o + jnp.dot(q_e, state.astype(BF16), preferred_element_type=F32)
            dcol = lax.dot_general(log_a, ones_c, tn_dims, precision=HIGHEST, preferred_element_type=F32)
            decay = jnp.exp(dcol)
            decay = jnp.concatenate([decay] * (GLA_HEAD_DV // LANES), axis=1)
            kv = lax.dot_general(k_end, v, tn_dims, preferred_element_type=F32)
            s_ref[h] = decay * state + kv
            o = (o * lax.rsqrt(jnp.mean(o * o, axis=-1, keepdims=True) + NORM_EPS)) * gn_ref[...]
            r = r_ref[rows, vc]
            y_ref[rows, vc] = (o * (r * jax.nn.sigmoid(r))).astype(BF16)
        return carry

    lax.fori_loop(0, n_chunks, chunk, 0)


def _gla(z, lr, wa2p, ba, gn, *, tb):
    t = z.shape[0]
    kw, vw = GLA_KEY_WIDTH, GLA_WIDTH
    return pl.pallas_call(
        functools.partial(_gla_body, n_chunks=tb // GLA_CHUNK),
        grid=(t // tb,),
        in_specs=[
            pl.BlockSpec((tb, kw), lambda i: (i, 4)),
            pl.BlockSpec((tb, kw), lambda i: (i, 5)),
            pl.BlockSpec((tb, vw), lambda i: (i, 3)),
            pl.BlockSpec((tb, vw), lambda i: (i, 4)),
            pl.BlockSpec((tb, LANES), lambda i: (i, 0)),
            pl.BlockSpec((LANES, kw), lambda i: (0, 0)),
            pl.BlockSpec((1, kw), lambda i: (0, 0)),
            pl.BlockSpec((1, GLA_HEAD_DV), lambda i: (0, 0)),
        ],
        out_specs=pl.BlockSpec((tb, vw), lambda i: (i, 0)),
        out_shape=jax.ShapeDtypeStruct((t, vw), BF16),
        scratch_shapes=[pltpu.VMEM((GLA_HEADS, GLA_HEAD_DK, GLA_HEAD_DV), F32)],
        compiler_params=_params(("arbitrary",)),
        name="gla",
    )(z, z, z, z, lr, wa2p, ba, gn)


def _bf16_bits(x):
    return lax.bitcast_convert_type(x.astype(BF16).astype(F32), U32)


def _outproj_body(ya_ref, yb_ref, x_ref, woa_ref, wob_ref, g2_ref, wr_ref, br_ref,
                  h1_ref, hp_ref, gate_ref, eid_ref, hacc_ref, *, nj, tn):
    j = pl.program_id(1)
    tm = x_ref.shape[0]
    acc = jnp.dot(ya_ref[...], woa_ref[...], preferred_element_type=F32)
    acc = acc + jnp.dot(yb_ref[...], wob_ref[...], preferred_element_type=F32)
    h = x_ref[...] + acc
    h1_ref[...] = h
    hacc_ref[j] = h

    @pl.when(j == nj - 1)
    def _():
        ssq = jnp.sum(hacc_ref[0] * hacc_ref[0], axis=-1, keepdims=True)
        for jj in range(1, nj):
            ssq = ssq + jnp.sum(hacc_ref[jj] * hacc_ref[jj], axis=-1, keepdims=True)
        rstd = lax.rsqrt(ssq * (1.0 / D_MODEL) + NORM_EPS)
        logits = br_ref[...]
        half_blocks = nj // 2
        for jj in range(half_blocks):
            lo = (hacc_ref[jj] * rstd) * g2_ref[:, jj * tn:(jj + 1) * tn]
            hi = (hacc_ref[jj + half_blocks] * rstd) * g2_ref[:, HALF + jj * tn:HALF + (jj + 1) * tn]
            logits = logits + jnp.dot(lo, wr_ref[jj * tn:(jj + 1) * tn, :],
                                      precision=HIGHEST, preferred_element_type=F32)
            logits = logits + jnp.dot(hi, wr_ref[HALF + jj * tn:HALF + (jj + 1) * tn, :],
                                      precision=HIGHEST, preferred_element_type=F32)
            packed = (_bf16_bits(hi) & jnp.uint32(0xFFFF0000)) | (_bf16_bits(lo) >> 16)
            for c in range(tn // LANES):
                ct = jj * (tn // LANES) + c
                hp_ref[pl.ds(ct, tm, stride=ROW_TILES), :] = packed[:, c * LANES:(c + 1) * LANES]

        lane = lax.broadcasted_iota(I32, logits.shape, 1)
        neg = jnp.float32(-jnp.inf)
        big = jnp.int32(LANES)
        cmask = lane < N_GROUPS
        cl = jnp.where(cmask, logits, neg)
        cmax = jnp.max(cl, axis=-1, keepdims=True)
        g_sel = jnp.min(jnp.where(cl == cmax, lane, big), axis=-1, keepdims=True)
        p_sel = 1.0 / jnp.sum(jnp.where(cmask, jnp.exp(logits - cmax), 0.0), axis=-1, keepdims=True)
        f0 = N_GROUPS + g_sel * EXPERTS_PER_GROUP
        fl = jnp.where((lane >= f0) & (lane < f0 + EXPERTS_PER_GROUP), logits, neg)
        v1 = jnp.max(fl, axis=-1, keepdims=True)
        i1 = jnp.min(jnp.where(fl == v1, lane, big), axis=-1, keepdims=True)
        fl2 = jnp.where(lane == i1, neg, fl)
        v2 = jnp.max(fl2, axis=-1, keepdims=True)
        i2 = jnp.min(jnp.where(fl2 == v2, lane, big), axis=-1, keepdims=True)
        e2 = jnp.exp(v2 - v1)
        w1 = p_sel / (1.0 + e2)
        w2 = p_sel * e2 / (1.0 + e2)
        gate_ref[...] = jnp.where(lane == 0, w1, jnp.where(lane == 1, w2, 0.0))
        eid_ref[...] = jnp.where(lane == 0, i1 - N_GROUPS, jnp.where(lane == 1, i2 - N_GROUPS, 0))


def _outproj(ya, yb, x, wo_a, wo_b, g2, wr, br, *, tm, tn):
    t = x.shape[0]
    nj = D_MODEL // tn
    return pl.pallas_call(
        functools.partial(_outproj_body, nj=nj, tn=tn),
        grid=(t // tm, nj),
        in_specs=[
            pl.BlockSpec((tm, GMLP_WIDTH), lambda i, j: (i, 0)),
            pl.BlockSpec((tm, GLA_WIDTH), lambda i, j: (i, 0)),
            pl.BlockSpec((tm, tn), lambda i, j: (i, j)),
            pl.BlockSpec((GMLP_WIDTH, tn), lambda i, j: (0, j)),
            pl.BlockSpec((GLA_WIDTH, tn), lambda i, j: (0, j)),
            pl.BlockSpec((1, D_MODEL), lambda i, j: (0, 0)),
            pl.BlockSpec((D_MODEL, LANES), lambda i, j: (0, 0)),
            pl.BlockSpec((1, LANES), lambda i, j: (0, 0)),
        ],
        out_specs=[
            pl.BlockSpec((tm, tn), lambda i, j: (i, j)),
            pl.BlockSpec((tm * ROW_TILES, LANES), lambda i, j: (i, 0)),
            pl.BlockSpec((tm, LANES), lambda i, j: (i, 0)),
            pl.BlockSpec((tm, LANES), lambda i, j: (i, 0)),
        ],
        out_shape=[
            jax.ShapeDtypeStruct((t, D_MODEL), F32),
            jax.ShapeDtypeStruct((t * ROW_TILES, LANES), U32),
            jax.ShapeDtypeStruct((t, LANES), F32),
            jax.ShapeDtypeStruct((t, LANES), I32),
        ],
        scratch_shapes=[pltpu.VMEM((nj, tm, tn), F32)],
        compiler_params=_params(("arbitrary", "arbitrary")),
        name="outproj_router",
    )(ya, yb, x, wo_a, wo_b, g2, wr, br)


FFN_PHASES = 4
FFN_MAIN = 512
DMA_UNROLL = 8


def _ffn_body(sbe_ref, sbr_ref, nsb_ref, tok0_ref, tokn_ref, hp_hbm, wg_ref, wu_ref, wd_ref, ys_ref,
              g_ref, sem, xb_ref, a_ref, wgb_ref, wub_ref, wdb_ref, *, rows):
    s = pl.program_id(0)
    p = pl.program_id(1)
    nsb = nsb_ref[0]
    valid = s < nsb
    nrows = sbr_ref[s]
    slot = lax.rem(s, 2)
    blocks = ((0, FFN_MAIN), (FFN_MAIN, rows - FFN_MAIN))

    def row_copy(tok_ref, slot_, i):
        src = hp_hbm.at[pl.ds(pl.multiple_of(tok_ref[0, 0, i] * ROW_TILES, ROW_TILES), ROW_TILES)]
        dst = g_ref.at[slot_, pl.ds(pl.multiple_of(i * ROW_TILES, ROW_TILES), ROW_TILES)]
        return pltpu.make_async_copy(src, dst, sem.at[slot_])

    def for_each_row(n_valid, fn):
        for r0, m in blocks:
            @pl.when(n_valid > r0)
            def _():
                def body(i, c):
                    fn(i)
                    return c
                lax.fori_loop(r0, r0 + m, body, 0, unroll=DMA_UNROLL)

    @pl.when(p == 0)
    def _():
        @pl.when(s == 0)
        def _():
            for_each_row(sbr_ref[0], lambda i: row_copy(tok0_ref, 0, i).start())

        @pl.when(valid)
        def _():
            for_each_row(nrows, lambda i: row_copy(tok0_ref, slot, i).wait())

        @pl.when(s + 1 < nsb)
        def _():
            nxt = jnp.minimum(s + 1, sbr_ref.shape[0] - 1)
            for_each_row(sbr_ref[nxt], lambda i: row_copy(tokn_ref, 1 - slot, i).start())

        for r0, m in blocks:
            @pl.when(valid & (nrows > r0))
            def _():
                for ct in range(ROW_TILES):
                    w = g_ref[slot, pl.ds(r0 * ROW_TILES + ct, m, stride=ROW_TILES), :]
                    cols = slice(ct * LANES, (ct + 1) * LANES)
                    xb_ref[r0:r0 + m, cols] = lax.bitcast_convert_type(w << 16, F32).astype(BF16)
                    cols = slice(HALF + ct * LANES, HALF + (ct + 1) * LANES)
                    xb_ref[r0:r0 + m, cols] = lax.bitcast_convert_type(w & jnp.uint32(0xFFFF0000), F32).astype(BF16)

    @pl.when(valid & (p < FFN_PHASES))
    def _():
        wgb_ref[...] = wg_ref[...].astype(BF16)
        wub_ref[...] = wu_ref[...].astype(BF16)
        for r0, m in blocks:
            @pl.when(nrows > r0)
            def _():
                xb = xb_ref[r0:r0 + m, :]
                g = jnp.dot(xb, wgb_ref[...], preferred_element_type=F32)
                u = jnp.dot(xb, wub_ref[...], preferred_element_type=F32)
                a_ref[p, r0:r0 + m, :] = ((g * jax.nn.sigmoid(g)) * u).astype(BF16)

    @pl.when(p >= FFN_PHASES)
    def _():
        @pl.when(valid)
        def _():
            wdb_ref[...] = wd_ref[...].astype(BF16)

        for r0, m in blocks:
            @pl.when(nrows > r0)
            def _():
                a = jnp.concatenate([a_ref[c, r0:r0 + m, :] for c in range(FFN_PHASES)], axis=1)
                ys_ref[r0:r0 + m, :] = jnp.dot(a, wdb_ref[...], preferred_element_type=F32)

            @pl.when(nrows <= r0)
            def _():
                ys_ref[r0:r0 + m, :] = jnp.zeros((m, ys_ref.shape[1]), F32)


def _ffn(sb_e, sb_rows, nsb, tok3, hp, w_gate, w_up, w_down, *, rows):
    s_max = sb_e.shape[0]
    fc = D_EXPERT // FFN_PHASES
    oc = D_MODEL // FFN_PHASES
    last = FFN_PHASES - 1

    def clamp_s(s, nsb):
        return jnp.minimum(s, nsb[0] - 1)

    def up_idx(s, p, sbe, sbr, nsb):
        return (sbe[clamp_s(s, nsb)], 0, jnp.where(s < nsb[0], jnp.minimum(p, last), last))

    def down_idx(s, p, sbe, sbr, nsb):
        prev = jnp.clip(s - 1, 0, nsb[0] - 1)
        in_down = (s < nsb[0]) & (p >= FFN_PHASES)
        return (jnp.where(in_down, sbe[clamp_s(s, nsb)], sbe[prev]), 0, jnp.where(in_down, p - FFN_PHASES, last))

    return pl.pallas_call(
        functools.partial(_ffn_body, rows=rows),
        grid_spec=pltpu.PrefetchScalarGridSpec(
            num_scalar_prefetch=3,
            grid=(s_max, 2 * FFN_PHASES),
            in_specs=[
                pl.BlockSpec((1, 1, rows), lambda s, p, sbe, sbr, nsb: (0, 0, 0), memory_space=pltpu.SMEM),
                pl.BlockSpec((1, 1, rows), lambda s, p, sbe, sbr, nsb: (jnp.minimum(s + 1, s_max - 1), 0, 0),
                             memory_space=pltpu.SMEM),
                pl.BlockSpec(memory_space=pl.ANY),
                pl.BlockSpec((None, D_MODEL, fc), up_idx),
                pl.BlockSpec((None, D_MODEL, fc), up_idx),
                pl.BlockSpec((None, D_EXPERT, oc), down_idx),
            ],
            out_specs=pl.BlockSpec((rows, oc), lambda s, p, sbe, sbr, nsb: (s, jnp.maximum(p - FFN_PHASES, 0))),
            scratch_shapes=[
                pltpu.VMEM((2, rows * ROW_TILES, LANES), U32),
                pltpu.SemaphoreType.DMA((2,)),
                pltpu.VMEM((rows, D_MODEL), BF16),
                pltpu.VMEM((FFN_PHASES, rows, fc), BF16),
                pltpu.VMEM((D_MODEL, fc), BF16),
                pltpu.VMEM((D_MODEL, fc), BF16),
                pltpu.VMEM((D_EXPERT, oc), BF16),
            ],
        ),
        out_shape=jax.ShapeDtypeStruct((s_max * rows, D_MODEL), F32),
        compiler_params=_params(("arbitrary", "arbitrary")),
        name="moe_ffn",
    )(sb_e, sb_rows, nsb, tok3, tok3, hp, w_gate, w_up, w_down)


def _combine_body(pos_ref, h1_ref, gate_ref, gf_ref, ys_hbm, out_ref, buf_ref, sem, *, tb):
    def row_copy(k, i):
        return pltpu.make_async_copy(ys_hbm.at[pl.ds(pos_ref[0, k, i], 1)], buf_ref.at[k, pl.ds(i, 1)], sem)

    def start(i, c):
        row_copy(0, i).start()
        row_copy(1, i).start()
        return c

    def wait(i, c):
        row_copy(0, i).wait()
        row_copy(1, i).wait()
        return c

    lax.fori_loop(0, tb, start, 0, unroll=DMA_UNROLL)
    lax.fori_loop(0, tb, wait, 0, unroll=DMA_UNROLL)
    gate = gate_ref[...]
    y = buf_ref[0] * gate[:, 0:1] + buf_ref[1] * gate[:, 1:2]
    h = h1_ref[...] + y
    ms = jnp.mean(h * h, axis=-1, keepdims=True)
    out_ref[...] = (h * lax.rsqrt(ms + NORM_EPS)) * gf_ref[...]


def _combine(pos3, h1, gate, gf, ys, *, tb):
    t = h1.shape[0]
    return pl.pallas_call(
        functools.partial(_combine_body, tb=tb),
        grid=(t // tb,),
        in_specs=[
            pl.BlockSpec((1, 2, tb), lambda i: (i, 0, 0), memory_space=pltpu.SMEM),
            pl.BlockSpec((tb, D_MODEL), lambda i: (i, 0)),
            pl.BlockSpec((tb, LANES), lambda i: (i, 0)),
            pl.BlockSpec((1, D_MODEL), lambda i: (0, 0)),
            pl.BlockSpec(memory_space=pl.ANY),
        ],
        out_specs=pl.BlockSpec((tb, D_MODEL), lambda i: (i, 0)),
        out_shape=jax.ShapeDtypeStruct((t, D_MODEL), F32),
        scratch_shapes=[pltpu.VMEM((2, tb, D_MODEL), F32), pltpu.SemaphoreType.DMA(())],
        compiler_params=_params(("arbitrary",)),
        name="moe_combine",
    )(pos3, h1, gate, gf, ys)


def _dispatch_plan(eid, *, rows, s_max):
    t = eid.shape[0]
    flat_e = eid[:, :2].reshape(-1)
    onehot = (flat_e[:, None] == jnp.arange(N_EXPERTS, dtype=I32)[None, :]).astype(I32)
    csum = jnp.cumsum(onehot, axis=0)
    rank = jnp.sum(csum * onehot, axis=1) - 1
    counts = csum[-1]
    nsb_e = (counts + rows - 1) // rows
    sb_end = jnp.cumsum(nsb_e)
    sb_start = sb_end - nsb_e
    nsb = sb_end[-1]
    pos = sb_start[flat_e] * rows + rank
    s_idx = jnp.minimum(jnp.arange(s_max, dtype=I32), nsb - 1)
    sb_e = jnp.minimum(jnp.searchsorted(sb_end, s_idx, side="right"), N_EXPERTS - 1).astype(I32)
    sb_rows = jnp.clip(counts[sb_e] - (s_idx - sb_start[sb_e]) * rows, 0, rows)
    sb_rows = jnp.where(jnp.arange(s_max) < nsb, sb_rows, 0).astype(I32)
    flat_tok = jnp.arange(2 * t, dtype=I32) // 2
    tok = jnp.zeros((s_max * rows,), I32).at[pos].set(flat_tok)
    return sb_e, sb_rows, nsb.reshape(1).astype(I32), tok.reshape(s_max, 1, rows), pos.reshape(t, 2)


def _layer(x, norm1_g, w_in, gmlp_v_g, gmlp_ws, gmlp_bs, gla_wa2, gla_ba, gla_norm_g, w_out, norm2_g,
           router_coarse_w, router_coarse_b, router_fine_w, router_fine_b, exp_w_gate, exp_w_up, exp_w_down,
           norm_f_g, *, tm_in, tb_gmlp, tb_gla, tm_out, tb_comb, sb_rows):
    t = x.shape[0]
    w_main = w_in[:, :D_PROJ_MAIN].astype(BF16)
    w_lr = jnp.pad(w_in[:, D_PROJ_MAIN:], ((0, 0), (0, LANES - GLA_GATE_RANK))).astype(BF16)
    wa2p = jnp.pad(gla_wa2, ((0, LANES - GLA_GATE_RANK), (0, 0)))
    wo = w_out.astype(BF16)
    n_route = N_GROUPS + N_EXPERTS
    wr = jnp.pad(jnp.concatenate([router_coarse_w, router_fine_w], axis=1), ((0, 0), (0, LANES - n_route)))
    br = jnp.pad(jnp.concatenate([router_coarse_b, router_fine_b]), (0, LANES - n_route)).reshape(1, LANES)

    z, lr = _inproj(x, norm1_g.reshape(1, -1), w_main, w_lr, tm=tm_in, tn=1024)
    y_a = _gmlp(z, gmlp_v_g.reshape(1, -1), gmlp_ws, gmlp_bs.T, tb=tb_gmlp)
    y_b = _gla(z, lr, wa2p, gla_ba.reshape(1, -1), gla_norm_g.reshape(1, -1), tb=tb_gla)
    h1, hp, gate, eid = _outproj(y_a, y_b, x, wo[:GMLP_WIDTH], wo[GMLP_WIDTH:], norm2_g.reshape(1, -1),
                                 wr, br, tm=tm_out, tn=512)

    s_max = N_EXPERTS + (2 * t) // sb_rows
    sb_e, sb_n, nsb, tok3, pos = _dispatch_plan(eid, rows=sb_rows, s_max=s_max)
    ys = _ffn(sb_e, sb_n, nsb, tok3, hp, exp_w_gate, exp_w_up, exp_w_down, rows=sb_rows)
    pos3 = pos.reshape(t // tb_comb, tb_comb, 2).transpose(0, 2, 1)
    return _combine(pos3, h1, gate, norm_f_g.reshape(1, -1), ys, tb=tb_comb)


def kernel(x, norm1_g, w_in, gmlp_v_g, gmlp_ws, gmlp_bs, gla_wa2, gla_ba, gla_norm_g, w_out, norm2_g,
           router_coarse_w, router_coarse_b, router_fine_w, router_fine_b, exp_w_gate, exp_w_up, exp_w_down,
           norm_f_g):
    b, t, d = x.shape
    assert b == 1 and d == D_MODEL and norm1_g.shape[0] == 1, "one sequence, one layer, the stated widths"

    def first(a):
        return a.reshape(a.shape[1:])

    out = _layer(
        first(x), first(norm1_g), first(w_in), first(gmlp_v_g), first(gmlp_ws), first(gmlp_bs), first(gla_wa2),
        first(gla_ba), first(gla_norm_g), first(w_out), first(norm2_g), first(router_coarse_w),
        first(router_coarse_b), first(router_fine_w), first(router_fine_b), first(exp_w_gate), first(exp_w_up),
        first(exp_w_down), norm_f_g,
        tm_in=512, tb_gmlp=256, tb_gla=512, tm_out=512, tb_comb=256, sb_rows=640)
    return out.reshape(b, t, d)
```

```python
import functools

import jax
import jax.numpy as jnp
from jax import lax
from jax.experimental import pallas as pl
from jax.experimental.pallas import tpu as pltpu

F32 = jnp.float32
BF16 = jnp.bfloat16
U32 = jnp.uint32
I32 = jnp.int32

D_MODEL = 4096
GMLP_WIDTH = 2048
GMLP_HEADS = 8
GMLP_HEAD_DIM = GMLP_WIDTH // GMLP_HEADS
GMLP_CHUNK = 128
GLA_WIDTH = 2048
GLA_HEADS = 4
GLA_KEY_WIDTH = 1024
GLA_HEAD_DK = GLA_KEY_WIDTH // GLA_HEADS
GLA_HEAD_DV = GLA_WIDTH // GLA_HEADS
GLA_GATE_RANK = 16
GLA_GATE_NORMALIZER = 16.0
GLA_CHUNK = 64
N_GROUPS = 8
EXPERTS_PER_GROUP = 8
N_EXPERTS = N_GROUPS * EXPERTS_PER_GROUP
D_EXPERT = D_MODEL // 4
NORM_EPS = 1e-6
D_PROJ_MAIN = 2 * GMLP_WIDTH + 2 * GLA_KEY_WIDTH + 2 * GLA_WIDTH

LANES = 128
HALF = D_MODEL // 2
ROW_TILES = HALF // LANES
FFN_KC = 1024
VMEM_LIMIT = 58 * 1024 * 1024

HIGHEST = lax.Precision.HIGHEST


def _params(semantics, vmem=VMEM_LIMIT):
    return pltpu.CompilerParams(dimension_semantics=semantics, vmem_limit_bytes=vmem)


def _inproj_body(x_ref, g_ref, w_ref, wlr_ref, z_ref, lr_ref, xn_ref):
    @pl.when(pl.program_id(1) == 0)
    def _():
        x = x_ref[...]
        ms = jnp.mean(x * x, axis=-1, keepdims=True)
        xn = ((x * lax.rsqrt(ms + NORM_EPS)) * g_ref[...]).astype(BF16)
        xn_ref[...] = xn
        lr_ref[...] = jnp.dot(xn, wlr_ref[...], preferred_element_type=F32)

    z_ref[...] = jnp.dot(xn_ref[...], w_ref[...], preferred_element_type=F32)


def _inproj(x, g, w_main, w_lr, *, tm, tn):
    t = x.shape[0]
    return pl.pallas_call(
        _inproj_body,
        grid=(t // tm, D_PROJ_MAIN // tn),
        in_specs=[
            pl.BlockSpec((tm, D_MODEL), lambda i, j: (i, 0)),
            pl.BlockSpec((1, D_MODEL), lambda i, j: (0, 0)),
            pl.BlockSpec((D_MODEL, tn), lambda i, j: (0, j)),
            pl.BlockSpec((D_MODEL, LANES), lambda i, j: (0, 0)),
        ],
        out_specs=[
            pl.BlockSpec((tm, tn), lambda i, j: (i, j)),
            pl.BlockSpec((tm, LANES), lambda i, j: (i, 0)),
        ],
        out_shape=[
            jax.ShapeDtypeStruct((t, D_PROJ_MAIN), F32),
            jax.ShapeDtypeStruct((t, LANES), F32),
        ],
        scratch_shapes=[pltpu.VMEM((tm, D_MODEL), BF16)],
        compiler_params=_params(("arbitrary", "arbitrary")),
        name="inproj",
    )(x, g, w_main, w_lr)


def _gelu(x):
    return x * (lax.erf(x * (2.0 ** -0.5)) + 1.0) * 0.5


def _gmlp_body(u_ref, v_ref, vg_ref, ws_ref, bst_ref, y_ref, *, n_chunks):
    row = lax.broadcasted_iota(I32, (GMLP_CHUNK, GMLP_CHUNK), 0)
    col = lax.broadcasted_iota(I32, (GMLP_CHUNK, GMLP_CHUNK), 1)
    causal = row >= col
    for c in range(n_chunks):
        rows = slice(c * GMLP_CHUNK, (c + 1) * GMLP_CHUNK)
        gv = _gelu(v_ref[rows, :])
        mu = jnp.mean(gv, axis=-1, keepdims=True)
        vc = gv - mu
        var = jnp.mean(vc * vc, axis=-1, keepdims=True)
        vn = ((vc * lax.rsqrt(var + NORM_EPS)) * vg_ref[...]).astype(BF16)
        for h in range(GMLP_HEADS):
            cols = slice(h * GMLP_HEAD_DIM, (h + 1) * GMLP_HEAD_DIM)
            wm = jnp.where(causal, ws_ref[h], 0.0).astype(BF16)
            s = jnp.dot(wm, vn[:, cols], preferred_element_type=F32) + bst_ref[:, h:h + 1]
            y_ref[rows, cols] = (_gelu(u_ref[rows, cols]) * s).astype(BF16)


def _gmlp(z, v_g, ws, bs_t, *, tb):
    t = z.shape[0]
    blk = GMLP_WIDTH
    return pl.pallas_call(
        functools.partial(_gmlp_body, n_chunks=tb // GMLP_CHUNK),
        grid=(t // tb,),
        in_specs=[
            pl.BlockSpec((tb, blk), lambda i: (i, 0)),
            pl.BlockSpec((tb, blk), lambda i: (i, 1)),
            pl.BlockSpec((1, blk), lambda i: (0, 0)),
            pl.BlockSpec((GMLP_HEADS, GMLP_CHUNK, GMLP_CHUNK), lambda i: (0, 0, 0)),
            pl.BlockSpec((GMLP_CHUNK, GMLP_HEADS), lambda i: (0, 0)),
        ],
        out_specs=pl.BlockSpec((tb, blk), lambda i: (i, 0)),
        out_shape=jax.ShapeDtypeStruct((t, blk), BF16),
        compiler_params=_params(("arbitrary",)),
        name="gmlp",
    )(z, z, v_g, ws, bs_t)


def _gla_body(q_ref, k_ref, v_ref, r_ref, lr_ref, wa2_ref, ba_ref, gn_ref, y_ref, s_ref, *, n_chunks):
    @pl.when(pl.program_id(0) == 0)
    def _():
        s_ref[...] = jnp.zeros_like(s_ref)

    c_len = GLA_CHUNK
    row = lax.broadcasted_iota(I32, (c_len, c_len), 0)
    col = lax.broadcasted_iota(I32, (c_len, c_len), 1)
    causal = row >= col
    ltri = jnp.where(causal, 1.0, 0.0).astype(F32)
    ones_c = jnp.ones((c_len, LANES), F32)
    tn_dims = (((0,), (0,)), ((), ()))
    nt_dims = (((1,), (1,)), ((), ()))

    def chunk(c, carry):
        t0 = pl.multiple_of(c * c_len, c_len)
        rows = pl.ds(t0, c_len)
        lr = lr_ref[rows, :]
        for h in range(GLA_HEADS):
            kc = slice(h * GLA_HEAD_DK, (h + 1) * GLA_HEAD_DK)
            vc = slice(h * GLA_HEAD_DV, (h + 1) * GLA_HEAD_DV)
            logit = jnp.dot(lr, wa2_ref[:, kc], precision=HIGHEST, preferred_element_type=F32) + ba_ref[:, kc]
            log_a = (jnp.minimum(logit, 0.0) - jnp.log1p(jnp.exp(-jnp.abs(logit)))) * (1.0 / GLA_GATE_NORMALIZER)
            bcum = jnp.dot(ltri, log_a, precision=HIGHEST, preferred_element_type=F32)
            b_last = bcum[c_len - 1:c_len, :]
            q = q_ref[rows, kc] * (GLA_HEAD_DK ** -0.5)
            k = k_ref[rows, kc]
            q_e = (q * jnp.exp(bcum)).astype(BF16)
            k_e = (k * jnp.exp(-bcum)).astype(BF16)
            k_end = (k * jnp.exp(b_last - bcum)).astype(BF16)
            v = v_ref[rows, vc].astype(BF16)
            scores = lax.dot_general(q_e, k_e, nt_dims, preferred_element_type=F32)
            scores = jnp.where(causal, scores, 0.0).astype(BF16)
            state = s_ref[h]
            o = jnp.dot(scores, v, preferred_element_type=F32)
            o = o + jnp.dot(q_e, state.astype(BF16), preferred_element_type=F32)
            dcol = lax.dot_general(log_a, ones_c, tn_dims, precision=HIGHEST, preferred_element_type=F32)
            decay = jnp.exp(dcol)
            decay = jnp.concatenate([decay] * (GLA_HEAD_DV // LANES), axis=1)
            kv = lax.dot_general(k_end, v, tn_dims, preferred_element_type=F32)
            s_ref[h] = decay * state + kv
            o = (o * lax.rsqrt(jnp.mean(o * o, axis=-1, keepdims=True) + NORM_EPS)) * gn_ref[...]
            r = r_ref[rows, vc]
            y_ref[rows, vc] = (o * (r * jax.nn.sigmoid(r))).astype(BF16)
        return carry

    lax.fori_loop(0, n_chunks, chunk, 0)


def _gla(z, lr, wa2p, ba, gn, *, tb):
    t = z.shape[0]
    kw, vw = GLA_KEY_WIDTH, GLA_WIDTH
    return pl.pallas_call(
        functools.partial(_gla_body, n_chunks=tb // GLA_CHUNK),
        grid=(t // tb,),
        in_specs=[
            pl.BlockSpec((tb, kw), lambda i: (i, 4)),
            pl.BlockSpec((tb, kw), lambda i: (i, 5)),
            pl.BlockSpec((tb, vw), lambda i: (i, 3)),
            pl.BlockSpec((tb, vw), lambda i: (i, 4)),
            pl.BlockSpec((tb, LANES), lambda i: (i, 0)),
            pl.BlockSpec((LANES, kw), lambda i: (0, 0)),
            pl.BlockSpec((1, kw), lambda i: (0, 0)),
            pl.BlockSpec((1, GLA_HEAD_DV), lambda i: (0, 0)),
        ],
        out_specs=pl.BlockSpec((tb, vw), lambda i: (i, 0)),
        out_shape=jax.ShapeDtypeStruct((t, vw), BF16),
        scratch_shapes=[pltpu.VMEM((GLA_HEADS, GLA_HEAD_DK, GLA_HEAD_DV), F32)],
        compiler_params=_params(("arbitrary",)),
        name="gla",
    )(z, z, z, z, lr, wa2p, ba, gn)


def _bf16_bits(x):
    return lax.bitcast_convert_type(x.astype(BF16).astype(F32), U32)


def _outproj_body(ya_ref, yb_ref, x_ref, woa_ref, wob_ref, g2_ref, wr_ref, br_ref,
                  h1_ref, hp_ref, gate_ref, eid_ref, hacc_ref, *, nj, tn):
    j = pl.program_id(1)
    tm = x_ref.shape[0]
    acc = jnp.dot(ya_ref[...], woa_ref[...], preferred_element_type=F32)
    acc = acc + jnp.dot(yb_ref[...], wob_ref[...], preferred_element_type=F32)
    h = x_ref[...] + acc
    h1_ref[...] = h
    hacc_ref[j] = h

    @pl.when(j == nj - 1)
    def _():
        ssq = jnp.sum(hacc_ref[0] * hacc_ref[0], axis=-1, keepdims=True)
        for jj in range(1, nj):
            ssq = ssq + jnp.sum(hacc_ref[jj] * hacc_ref[jj], axis=-1, keepdims=True)
        rstd = lax.rsqrt(ssq * (1.0 / D_MODEL) + NORM_EPS)
        logits = br_ref[...]

        def normed(jj):
            cols = slice(jj * tn, (jj + 1) * tn)
            hn = (hacc_ref[jj] * rstd) * g2_ref[:, cols]
            return hn, jnp.dot(hn, wr_ref[cols, :], precision=HIGHEST, preferred_element_type=F32)

        kb = FFN_KC // tn
        for wb in range(nj // 2):
            q, hh = divmod(wb, kb)
            lo, d_lo = normed(2 * q * kb + hh)
            hi, d_hi = normed((2 * q + 1) * kb + hh)
            logits = logits + d_lo + d_hi
            packed = (_bf16_bits(hi) & jnp.uint32(0xFFFF0000)) | (_bf16_bits(lo) >> 16)
            for c in range(tn // LANES):
                ct = wb * (tn // LANES) + c
                hp_ref[pl.ds(ct, tm, stride=ROW_TILES), :] = packed[:, c * LANES:(c + 1) * LANES]

        lane = lax.broadcasted_iota(I32, logits.shape, 1)
        neg = jnp.float32(-jnp.inf)
        big = jnp.int32(LANES)
        cmask = lane < N_GROUPS
        cl = jnp.where(cmask, logits, neg)
        cmax = jnp.max(cl, axis=-1, keepdims=True)
        g_sel = jnp.min(jnp.where(cl == cmax, lane, big), axis=-1, keepdims=True)
        p_sel = 1.0 / jnp.sum(jnp.where(cmask, jnp.exp(logits - cmax), 0.0), axis=-1, keepdims=True)
        f0 = N_GROUPS + g_sel * EXPERTS_PER_GROUP
        fl = jnp.where((lane >= f0) & (lane < f0 + EXPERTS_PER_GROUP), logits, neg)
        v1 = jnp.max(fl, axis=-1, keepdims=True)
        i1 = jnp.min(jnp.where(fl == v1, lane, big), axis=-1, keepdims=True)
        fl2 = jnp.where(lane == i1, neg, fl)
        v2 = jnp.max(fl2, axis=-1, keepdims=True)
        i2 = jnp.min(jnp.where(fl2 == v2, lane, big), axis=-1, keepdims=True)
        e2 = jnp.exp(v2 - v1)
        w1 = p_sel / (1.0 + e2)
        w2 = p_sel * e2 / (1.0 + e2)
        gate_ref[...] = jnp.where(lane == 0, w1, jnp.where(lane == 1, w2, 0.0))
        eid_ref[...] = jnp.where(lane == 0, i1 - N_GROUPS, jnp.where(lane == 1, i2 - N_GROUPS, 0))


def _outproj(ya, yb, x, wo, g2, wr, br, *, tm, tn):
    t = x.shape[0]
    nj = D_MODEL // tn
    assert FFN_KC % tn == 0 and GMLP_WIDTH == GLA_WIDTH
    return pl.pallas_call(
        functools.partial(_outproj_body, nj=nj, tn=tn),
        grid=(t // tm, nj),
        in_specs=[
            pl.BlockSpec((tm, GMLP_WIDTH), lambda i, j: (i, 0)),
            pl.BlockSpec((tm, GLA_WIDTH), lambda i, j: (i, 0)),
            pl.BlockSpec((tm, tn), lambda i, j: (i, j)),
            pl.BlockSpec((GMLP_WIDTH, tn), lambda i, j: (0, j)),
            pl.BlockSpec((GLA_WIDTH, tn), lambda i, j: (1, j)),
            pl.BlockSpec((1, D_MODEL), lambda i, j: (0, 0)),
            pl.BlockSpec((D_MODEL, LANES), lambda i, j: (0, 0)),
            pl.BlockSpec((1, LANES), lambda i, j: (0, 0)),
        ],
        out_specs=[
            pl.BlockSpec((tm, tn), lambda i, j: (i, j)),
            pl.BlockSpec((tm * ROW_TILES, LANES), lambda i, j: (i, 0)),
            pl.BlockSpec((tm, LANES), lambda i, j: (i, 0)),
            pl.BlockSpec((tm, LANES), lambda i, j: (i, 0)),
        ],
        out_shape=[
            jax.ShapeDtypeStruct((t, D_MODEL), F32),
            jax.ShapeDtypeStruct((t * ROW_TILES, LANES), U32),
            jax.ShapeDtypeStruct((t, LANES), F32),
            jax.ShapeDtypeStruct((t, LANES), I32),
        ],
        scratch_shapes=[pltpu.VMEM((nj, tm, tn), F32)],
        compiler_params=_params(("arbitrary", "arbitrary")),
        name="outproj_router",
    )(ya, yb, x, wo, wo, g2, wr, br)


FFN_UP_STEPS = D_MODEL // FFN_KC
FFN_NC = 1024
FFN_DOWN_STEPS = D_MODEL // FFN_NC
ROW_UNIT = 128
DMA_UNROLL = 8


def _ffn_body(sbe_ref, sbr_ref, nsb_ref, tok0_ref, tokn_ref, hp_hbm, wg_ref, wu_ref, wd_ref, ys_ref,
              g_ref, sem, xh_ref, gacc_ref, uacc_ref, a_ref, wgb_ref, wub_ref, wdb_ref, *, rows):
    s = pl.program_id(0)
    p = pl.program_id(1)
    nsb = nsb_ref[0]
    valid = s < nsb
    slot = lax.rem(s, 2)
    word_tiles = FFN_KC // LANES

    def units(n_rows):
        return lax.shift_right_logical(n_rows + (ROW_UNIT - 1), ROW_UNIT.bit_length() - 1)

    n_units = units(sbr_ref[s])

    def row_copy(tok_ref, slot_, i):
        src = hp_hbm.at[pl.ds(pl.multiple_of(tok_ref[0, 0, i] * ROW_TILES, ROW_TILES), ROW_TILES)]
        dst = g_ref.at[slot_, pl.ds(pl.multiple_of(i * ROW_TILES, ROW_TILES), ROW_TILES)]
        return pltpu.make_async_copy(src, dst, sem.at[slot_])

    def for_each_row(n_units_, fn):
        def unit_body(u, c):
            def body(i, c2):
                fn(u * ROW_UNIT + i)
                return c2
            return lax.fori_loop(0, ROW_UNIT, body, c, unroll=DMA_UNROLL)
        lax.fori_loop(0, n_units_, unit_body, 0)

    def for_row_blocks(fn):
        pair = 2 * ROW_UNIT
        for j in range(rows // pair):
            @pl.when(n_units >= 2 * j + 2)
            def _():
                fn(pair * j, pair)

            @pl.when(n_units == 2 * j + 1)
            def _():
                fn(pair * j, ROW_UNIT)
        if rows % pair:
            @pl.when(n_units == rows // ROW_UNIT)
            def _():
                fn(rows - ROW_UNIT, ROW_UNIT)

    @pl.when(p == 0)
    def _():
        @pl.when(s == 0)
        def _():
            for_each_row(units(sbr_ref[0]), lambda i: row_copy(tok0_ref, 0, i).start())

        @pl.when(valid)
        def _():
            for_each_row(n_units, lambda i: row_copy(tok0_ref, slot, i).wait())

        @pl.when(s + 1 < nsb)
        def _():
            nxt = jnp.minimum(s + 1, sbr_ref.shape[0] - 1)
            for_each_row(units(sbr_ref[nxt]), lambda i: row_copy(tokn_ref, 1 - slot, i).start())

    for step in range(FFN_UP_STEPS):
        @pl.when(valid & (p == step))
        def _(step=step):
            wgb_ref[...] = wg_ref[...].astype(BF16)
            wub_ref[...] = wu_ref[...].astype(BF16)

            def block(r0, m):
                r = slice(r0, r0 + m)
                if step % 2 == 0:
                    first = r0 * ROW_TILES + (step // 2) * word_tiles
                    w = jnp.concatenate(
                        [g_ref[slot, pl.ds(first + c, m, stride=ROW_TILES), :] for c in range(word_tiles)], axis=1)
                    xk = lax.bitcast_convert_type(w << 16, F32).astype(BF16)
                    xh_ref[r, :] = lax.bitcast_convert_type(w & jnp.uint32(0xFFFF0000), F32).astype(BF16)
                else:
                    xk = xh_ref[r, :]
                g = jnp.dot(xk, wgb_ref[...], preferred_element_type=F32)
                u = jnp.dot(xk, wub_ref[...], preferred_element_type=F32)
                if step > 0:
                    g = gacc_ref[r, :] + g
                    u = uacc_ref[r, :] + u
                if step < FFN_UP_STEPS - 1:
                    gacc_ref[r, :] = g
                    uacc_ref[r, :] = u
                else:
                    a_ref[r, :] = ((g * jax.nn.sigmoid(g)) * u).astype(BF16)

            for_row_blocks(block)

    @pl.when(p >= FFN_UP_STEPS)
    def _():
        @pl.when(valid)
        def _():
            wdb_ref[...] = wd_ref[...].astype(BF16)

        def block(r0, m):
            ys_ref[r0:r0 + m, :] = jnp.dot(a_ref[r0:r0 + m, :], wdb_ref[...], preferred_element_type=F32)

        for_row_blocks(block)
        for k in range(rows // ROW_UNIT):
            @pl.when(n_units <= k)
            def _(k=k):
                ys_ref[k * ROW_UNIT:(k + 1) * ROW_UNIT, :] = jnp.zeros((ROW_UNIT, ys_ref.shape[1]), F32)


def _ffn(sb_e, sb_rows, nsb, tok3, hp, w_gate, w_up, w_down, *, rows):
    s_max = sb_e.shape[0]
    assert rows % ROW_UNIT == 0
    n_steps = FFN_UP_STEPS + FFN_DOWN_STEPS
    last_up = FFN_UP_STEPS - 1
    last_down = FFN_DOWN_STEPS - 1

    def gate_idx(s, p, sbe, sbr, nsb):
        last_s = nsb[0] - 1
        ahead = (p == n_steps - 1) & (s < last_s)
        e = jnp.where(ahead, sbe[jnp.minimum(s + 1, last_s)], sbe[jnp.minimum(s, last_s)])
        k = jnp.where(ahead, 0, jnp.where(s <= last_s, jnp.minimum(p, last_up), last_up))
        return (e, k, 0)

    def up_idx(s, p, sbe, sbr, nsb):
        last_s = nsb[0] - 1
        return (sbe[jnp.minimum(s, last_s)], jnp.where(s <= last_s, jnp.minimum(p, last_up), last_up), 0)

    def down_idx(s, p, sbe, sbr, nsb):
        last_s = nsb[0] - 1
        prev = jnp.clip(s - 1, 0, last_s)
        in_down = (s <= last_s) & (p >= FFN_UP_STEPS)
        return (jnp.where(in_down, sbe[jnp.minimum(s, last_s)], sbe[prev]), 0,
                jnp.where(in_down, p - FFN_UP_STEPS, last_down))

    return pl.pallas_call(
        functools.partial(_ffn_body, rows=rows),
        grid_spec=pltpu.PrefetchScalarGridSpec(
            num_scalar_prefetch=3,
            grid=(s_max, n_steps),
            in_specs=[
                pl.BlockSpec((1, 1, rows), lambda s, p, sbe, sbr, nsb: (0, 0, 0), memory_space=pltpu.SMEM),
                pl.BlockSpec((1, 1, rows), lambda s, p, sbe, sbr, nsb: (jnp.minimum(s + 1, s_max - 1), 0, 0),
                             memory_space=pltpu.SMEM),
                pl.BlockSpec(memory_space=pl.ANY),
                pl.BlockSpec((None, FFN_KC, D_EXPERT), gate_idx),
                pl.BlockSpec((None, FFN_KC, D_EXPERT), up_idx),
                pl.BlockSpec((None, D_EXPERT, FFN_NC), down_idx),
            ],
            out_specs=pl.BlockSpec((rows, FFN_NC),
                                   lambda s, p, sbe, sbr, nsb: (s, jnp.maximum(p - FFN_UP_STEPS, 0))),
            scratch_shapes=[
                pltpu.VMEM((2, rows * ROW_TILES, LANES), U32),
                pltpu.SemaphoreType.DMA((2,)),
                pltpu.VMEM((rows, FFN_KC), BF16),
                pltpu.VMEM((rows, D_EXPERT), F32),
                pltpu.VMEM((rows, D_EXPERT), F32),
                pltpu.VMEM((rows, D_EXPERT), BF16),
                pltpu.VMEM((FFN_KC, D_EXPERT), BF16),
                pltpu.VMEM((FFN_KC, D_EXPERT), BF16),
                pltpu.VMEM((D_EXPERT, FFN_NC), BF16),
            ],
        ),
        out_shape=jax.ShapeDtypeStruct((s_max * rows, D_MODEL), F32),
        compiler_params=_params(("arbitrary", "arbitrary")),
        name="moe_ffn",
    )(sb_e, sb_rows, nsb, tok3, tok3, hp, w_gate, w_up, w_down)


def _combine_body(pos_ref, h1_ref, gate_ref, gf_ref, ys_hbm, out_ref, buf_ref, sem, *, tb):
    def row_copy(k, i):
        return pltpu.make_async_copy(ys_hbm.at[pl.ds(pos_ref[0, k, i], 1)], buf_ref.at[k, pl.ds(i, 1)], sem)

    def start(i, c):
        row_copy(0, i).start()
        row_copy(1, i).start()
        return c

    def wait(i, c):
        row_copy(0, i).wait()
        row_copy(1, i).wait()
        return c

    lax.fori_loop(0, tb, start, 0, unroll=DMA_UNROLL)
    lax.fori_loop(0, tb, wait, 0, unroll=DMA_UNROLL)
    gate = gate_ref[...]
    y = buf_ref[0] * gate[:, 0:1] + buf_ref[1] * gate[:, 1:2]
    h = h1_ref[...] + y
    ms = jnp.mean(h * h, axis=-1, keepdims=True)
    out_ref[...] = (h * lax.rsqrt(ms + NORM_EPS)) * gf_ref[...]


def _combine(pos3, h1, gate, gf, ys, *, tb):
    t = h1.shape[0]
    return pl.pallas_call(
        functools.partial(_combine_body, tb=tb),
        grid=(t // tb,),
        in_specs=[
            pl.BlockSpec((1, 2, tb), lambda i: (i, 0, 0), memory_space=pltpu.SMEM),
            pl.BlockSpec((tb, D_MODEL), lambda i: (i, 0)),
            pl.BlockSpec((tb, LANES), lambda i: (i, 0)),
            pl.BlockSpec((1, D_MODEL), lambda i: (0, 0)),
            pl.BlockSpec(memory_space=pl.ANY),
        ],
        out_specs=pl.BlockSpec((tb, D_MODEL), lambda i: (i, 0)),
        out_shape=jax.ShapeDtypeStruct((t, D_MODEL), F32),
        scratch_shapes=[pltpu.VMEM((2, tb, D_MODEL), F32), pltpu.SemaphoreType.DMA(())],
        compiler_params=_params(("arbitrary",)),
        name="moe_combine",
    )(pos3, h1, gate, gf, ys)


def _dispatch_plan(eid, *, rows, s_max):
    t = eid.shape[0]
    flat_e = eid[:, :2].reshape(-1)
    onehot = (flat_e[:, None] == jnp.arange(N_EXPERTS, dtype=I32)[None, :]).astype(I32)
    csum = jnp.cumsum(onehot, axis=0)
    rank = jnp.sum(csum * onehot, axis=1) - 1
    counts = csum[-1]
    nsb_e = (counts + rows - 1) // rows
    sb_end = jnp.cumsum(nsb_e)
    sb_start = sb_end - nsb_e
    nsb = sb_end[-1]
    pos = sb_start[flat_e] * rows + rank
    s_idx = jnp.minimum(jnp.arange(s_max, dtype=I32), nsb - 1)
    sb_e = jnp.minimum(jnp.searchsorted(sb_end, s_idx, side="right"), N_EXPERTS - 1).astype(I32)
    sb_rows = jnp.clip(counts[sb_e] - (s_idx - sb_start[sb_e]) * rows, 0, rows)
    sb_rows = jnp.where(jnp.arange(s_max) < nsb, sb_rows, 0).astype(I32)
    flat_tok = jnp.arange(2 * t, dtype=I32) // 2
    tok = jnp.zeros((s_max * rows,), I32).at[pos].set(flat_tok)
    return sb_e, sb_rows, nsb.reshape(1).astype(I32), tok.reshape(s_max, 1, rows), pos.reshape(t, 2)


def _layer(x, norm1_g, w_in, gmlp_v_g, gmlp_ws, gmlp_bs, gla_wa2, gla_ba, gla_norm_g, w_out, norm2_g,
           router_coarse_w, router_coarse_b, router_fine_w, router_fine_b, exp_w_gate, exp_w_up, exp_w_down,
           norm_f_g, *, tm_in, tb_gmlp, tb_gla, tm_out, tb_comb, sb_rows):
    t = x.shape[0]
    w_main = w_in.astype(BF16)
    w_lr = jnp.pad(w_in[:, D_PROJ_MAIN:], ((0, 0), (0, LANES - GLA_GATE_RANK))).astype(BF16)
    wa2p = jnp.pad(gla_wa2, ((0, LANES - GLA_GATE_RANK), (0, 0)))
    wo = w_out.astype(BF16)
    n_route = N_GROUPS + N_EXPERTS
    wr = jnp.pad(jnp.concatenate([router_coarse_w, router_fine_w], axis=1), ((0, 0), (0, LANES - n_route)))
    br = jnp.pad(jnp.concatenate([router_coarse_b, router_fine_b]), (0, LANES - n_route)).reshape(1, LANES)

    z, lr = _inproj(x, norm1_g.reshape(1, -1), w_main, w_lr, tm=tm_in, tn=1024)
    y_a = _gmlp(z, gmlp_v_g.reshape(1, -1), gmlp_ws, gmlp_bs.T, tb=tb_gmlp)
    y_b = _gla(z, lr, wa2p, gla_ba.reshape(1, -1), gla_norm_g.reshape(1, -1), tb=tb_gla)
    h1, hp, gate, eid = _outproj(y_a, y_b, x, wo, norm2_g.reshape(1, -1), wr, br, tm=tm_out, tn=512)

    s_max = N_EXPERTS + (2 * t) // sb_rows
    sb_e, sb_n, nsb, tok3, pos = _dispatch_plan(eid, rows=sb_rows, s_max=s_max)
    ys = _ffn(sb_e, sb_n, nsb, tok3, hp, exp_w_gate, exp_w_up, exp_w_down, rows=sb_rows)
    pos3 = pos.reshape(t // tb_comb, tb_comb, 2).transpose(0, 2, 1)
    return _combine(pos3, h1, gate, norm_f_g.reshape(1, -1), ys, tb=tb_comb)


def kernel(x, norm1_g, w_in, gmlp_v_g, gmlp_ws, gmlp_bs, gla_wa2, gla_ba, gla_norm_g, w_out, norm2_g,
           router_coarse_w, router_coarse_b, router_fine_w, router_fine_b, exp_w_gate, exp_w_up, exp_w_down,
           norm_f_g):
    b, t, d = x.shape
    assert b == 1 and d == D_MODEL and norm1_g.shape[0] == 1, "one sequence, one layer, the stated widths"

    def first(a):
        return a.reshape(a.shape[1:])

    out = _layer(
        first(x), first(norm1_g), first(w_in), first(gmlp_v_g), first(gmlp_ws), first(gmlp_bs), first(gla_wa2),
        first(gla_ba), first(gla_norm_g), first(w_out), first(norm2_g), first(router_coarse_w),
        first(router_coarse_b), first(router_fine_w), first(router_fine_b), first(exp_w_gate), first(exp_w_up),
        first(exp_w_down), norm_f_g,
        tm_in=512, tb_gmlp=256, tb_gla=512, tm_out=512, tb_comb=256, sb_rows=640)
    return out.reshape(b, t, d)
```

```python
import functools

import jax
import jax.numpy as jnp
from jax import lax
from jax.experimental import pallas as pl
from jax.experimental.pallas import tpu as pltpu

F32 = jnp.float32
BF16 = jnp.bfloat16
U32 = jnp.uint32
I32 = jnp.int32

D_MODEL = 4096
GMLP_WIDTH = 2048
GMLP_HEADS = 8
GMLP_HEAD_DIM = GMLP_WIDTH // GMLP_HEADS
GMLP_CHUNK = 128
GLA_WIDTH = 2048
GLA_HEADS = 4
GLA_KEY_WIDTH = 1024
GLA_HEAD_DK = GLA_KEY_WIDTH // GLA_HEADS
GLA_HEAD_DV = GLA_WIDTH // GLA_HEADS
GLA_GATE_RANK = 16
GLA_GATE_NORMALIZER = 16.0
GLA_CHUNK = 64
N_GROUPS = 8
EXPERTS_PER_GROUP = 8
N_EXPERTS = N_GROUPS * EXPERTS_PER_GROUP
D_EXPERT = D_MODEL // 4
NORM_EPS = 1e-6
D_PROJ_MAIN = 2 * GMLP_WIDTH + 2 * GLA_KEY_WIDTH + 2 * GLA_WIDTH

LANES = 128
HALF = D_MODEL // 2
ROW_TILES = HALF // LANES
FFN_KC = 1024
VMEM_LIMIT = 58 * 1024 * 1024

HIGHEST = lax.Precision.HIGHEST


def _params(semantics, vmem=VMEM_LIMIT):
    return pltpu.CompilerParams(dimension_semantics=semantics, vmem_limit_bytes=vmem)


def _inproj_body(x_ref, g_ref, w_ref, wlr_ref, z_ref, lr_ref, xn_ref):
    @pl.when(pl.program_id(1) == 0)
    def _():
        x = x_ref[...]
        ms = jnp.mean(x * x, axis=-1, keepdims=True)
        xn = ((x * lax.rsqrt(ms + NORM_EPS)) * g_ref[...]).astype(BF16)
        xn_ref[...] = xn
        lr_ref[...] = jnp.dot(xn, wlr_ref[...], preferred_element_type=F32)

    z_ref[...] = jnp.dot(xn_ref[...], w_ref[...], preferred_element_type=F32)


def _inproj(x, g, w_main, w_lr, *, tm, tn):
    t = x.shape[0]
    return pl.pallas_call(
        _inproj_body,
        grid=(t // tm, D_PROJ_MAIN // tn),
        in_specs=[
            pl.BlockSpec((tm, D_MODEL), lambda i, j: (i, 0)),
            pl.BlockSpec((1, D_MODEL), lambda i, j: (0, 0)),
            pl.BlockSpec((D_MODEL, tn), lambda i, j: (0, j)),
            pl.BlockSpec((D_MODEL, LANES), lambda i, j: (0, 0)),
        ],
        out_specs=[
            pl.BlockSpec((tm, tn), lambda i, j: (i, j)),
            pl.BlockSpec((tm, LANES), lambda i, j: (i, 0)),
        ],
        out_shape=[
            jax.ShapeDtypeStruct((t, D_PROJ_MAIN), F32),
            jax.ShapeDtypeStruct((t, LANES), F32),
        ],
        scratch_shapes=[pltpu.VMEM((tm, D_MODEL), BF16)],
        compiler_params=_params(("arbitrary", "arbitrary")),
        name="inproj",
    )(x, g, w_main, w_lr)


def _gelu(x):
    return x * (lax.erf(x * (2.0 ** -0.5)) + 1.0) * 0.5


def _gmlp_body(u_ref, v_ref, vg_ref, ws_ref, bst_ref, y_ref, *, n_chunks):
    row = lax.broadcasted_iota(I32, (GMLP_CHUNK, GMLP_CHUNK), 0)
    col = lax.broadcasted_iota(I32, (GMLP_CHUNK, GMLP_CHUNK), 1)
    causal = row >= col
    for c in range(n_chunks):
        rows = slice(c * GMLP_CHUNK, (c + 1) * GMLP_CHUNK)
        gv = _gelu(v_ref[rows, :])
        mu = jnp.mean(gv, axis=-1, keepdims=True)
        vc = gv - mu
        var = jnp.mean(vc * vc, axis=-1, keepdims=True)
        vn = ((vc * lax.rsqrt(var + NORM_EPS)) * vg_ref[...]).astype(BF16)
        for h in range(GMLP_HEADS):
            cols = slice(h * GMLP_HEAD_DIM, (h + 1) * GMLP_HEAD_DIM)
            wm = jnp.where(causal, ws_ref[h], 0.0).astype(BF16)
            s = jnp.dot(wm, vn[:, cols], preferred_element_type=F32) + bst_ref[:, h:h + 1]
            y_ref[rows, cols] = (_gelu(u_ref[rows, cols]) * s).astype(BF16)


def _gmlp(z, v_g, ws, bs_t, *, tb):
    t = z.shape[0]
    blk = GMLP_WIDTH
    return pl.pallas_call(
        functools.partial(_gmlp_body, n_chunks=tb // GMLP_CHUNK),
        grid=(t // tb,),
        in_specs=[
            pl.BlockSpec((tb, blk), lambda i: (i, 0)),
            pl.BlockSpec((tb, blk), lambda i: (i, 1)),
            pl.BlockSpec((1, blk), lambda i: (0, 0)),
            pl.BlockSpec((GMLP_HEADS, GMLP_CHUNK, GMLP_CHUNK), lambda i: (0, 0, 0)),
            pl.BlockSpec((GMLP_CHUNK, GMLP_HEADS), lambda i: (0, 0)),
        ],
        out_specs=pl.BlockSpec((tb, blk), lambda i: (i, 0)),
        out_shape=jax.ShapeDtypeStruct((t, blk), BF16),
        compiler_params=_params(("arbitrary",)),
        name="gmlp",
    )(z, z, v_g, ws, bs_t)


def _gla_body(q_ref, k_ref, v_ref, r_ref, lr_ref, wa2_ref, ba_ref, gn_ref, y_ref, st_ref, bcum_ref, *, n_chunks):
    @pl.when(pl.program_id(0) == 0)
    def _():
        st_ref[...] = jnp.zeros_like(st_ref)

    c_len = GLA_CHUNK
    tb = lr_ref.shape[0]
    tn_dims = (((0,), (0,)), ((), ()))
    nt_dims = (((1,), (1,)), ((), ()))

    logit = jnp.dot(lr_ref[...].astype(BF16), wa2_ref[...], preferred_element_type=F32) + ba_ref[...]
    log_a = (jnp.minimum(logit, 0.0) - jnp.log1p(jnp.exp(-jnp.abs(logit)))) * (1.0 / GLA_GATE_NORMALIZER)
    row_b = lax.broadcasted_iota(I32, (tb, tb), 0)
    col_b = lax.broadcasted_iota(I32, (tb, tb), 1)
    shift = c_len.bit_length() - 1
    same_chunk = lax.shift_right_logical(row_b, shift) == lax.shift_right_logical(col_b, shift)
    lmat = jnp.where(same_chunk & (row_b >= col_b), 1.0, 0.0).astype(F32)
    bcum_ref[...] = jnp.dot(lmat, log_a, precision=HIGHEST, preferred_element_type=F32)

    row = lax.broadcasted_iota(I32, (c_len, c_len), 0)
    col = lax.broadcasted_iota(I32, (c_len, c_len), 1)
    causal = row >= col

    def chunk(c, carry):
        t0 = pl.multiple_of(c * c_len, c_len)
        rows = pl.ds(t0, c_len)
        for h in range(GLA_HEADS):
            kc = slice(h * GLA_HEAD_DK, (h + 1) * GLA_HEAD_DK)
            vc = slice(h * GLA_HEAD_DV, (h + 1) * GLA_HEAD_DV)
            bcum = bcum_ref[rows, kc]
            b_last = bcum[c_len - 1:c_len, :]
            q = q_ref[rows, kc] * (GLA_HEAD_DK ** -0.5)
            k = k_ref[rows, kc]
            q_e = (q * jnp.exp(bcum)).astype(BF16)
            k_e = (k * jnp.exp(-bcum)).astype(BF16)
            k_end = (k * jnp.exp(b_last - bcum)).astype(BF16)
            v = v_ref[rows, vc].astype(BF16)
            scores = lax.dot_general(q_e, k_e, nt_dims, preferred_element_type=F32)
            scores = jnp.where(causal, scores, 0.0).astype(BF16)
            state_t = st_ref[h]
            o = jnp.dot(scores, v, preferred_element_type=F32)
            o = o + lax.dot_general(q_e, state_t.astype(BF16), nt_dims, preferred_element_type=F32)
            kv_t = lax.dot_general(v, k_end, tn_dims, preferred_element_type=F32)
            st_ref[h] = state_t * jnp.exp(b_last) + kv_t
            o = (o * lax.rsqrt(jnp.mean(o * o, axis=-1, keepdims=True) + NORM_EPS)) * gn_ref[...]
            r = r_ref[rows, vc]
            y_ref[rows, vc] = (o * (r * jax.nn.sigmoid(r))).astype(BF16)
        return carry

    lax.fori_loop(0, n_chunks, chunk, 0)


def _gla(z, lr, wa2p, ba, gn, *, tb):
    t = z.shape[0]
    kw, vw = GLA_KEY_WIDTH, GLA_WIDTH
    return pl.pallas_call(
        functools.partial(_gla_body, n_chunks=tb // GLA_CHUNK),
        grid=(t // tb,),
        in_specs=[
            pl.BlockSpec((tb, kw), lambda i: (i, 4)),
            pl.BlockSpec((tb, kw), lambda i: (i, 5)),
            pl.BlockSpec((tb, vw), lambda i: (i, 3)),
            pl.BlockSpec((tb, vw), lambda i: (i, 4)),
            pl.BlockSpec((tb, LANES), lambda i: (i, 0)),
            pl.BlockSpec((LANES, kw), lambda i: (0, 0)),
            pl.BlockSpec((1, kw), lambda i: (0, 0)),
            pl.BlockSpec((1, GLA_HEAD_DV), lambda i: (0, 0)),
        ],
        out_specs=pl.BlockSpec((tb, vw), lambda i: (i, 0)),
        out_shape=jax.ShapeDtypeStruct((t, vw), BF16),
        scratch_shapes=[pltpu.VMEM((GLA_HEADS, GLA_HEAD_DV, GLA_HEAD_DK), F32),
                        pltpu.VMEM((tb, kw), F32)],
        compiler_params=_params(("arbitrary",)),
        name="gla",
    )(z, z, z, z, lr, wa2p, ba, gn)


def _bf16_bits(x):
    return lax.bitcast_convert_type(x.astype(BF16).astype(F32), U32)


def _outproj_body(ya_ref, yb_ref, x_ref, woa_ref, wob_ref, g2_ref, wr_ref, br_ref,
                  h1_ref, hp_ref, gate_ref, eid_ref, hacc_ref, *, nj, tn):
    j = pl.program_id(1)
    tm = x_ref.shape[0]
    acc = jnp.dot(ya_ref[...], woa_ref[...], preferred_element_type=F32)
    acc = acc + jnp.dot(yb_ref[...], wob_ref[...], preferred_element_type=F32)
    h = x_ref[...] + acc
    h1_ref[...] = h
    hacc_ref[j] = h

    @pl.when(j == nj - 1)
    def _():
        ssq = jnp.sum(hacc_ref[0] * hacc_ref[0], axis=-1, keepdims=True)
        for jj in range(1, nj):
            ssq = ssq + jnp.sum(hacc_ref[jj] * hacc_ref[jj], axis=-1, keepdims=True)
        rstd = lax.rsqrt(ssq * (1.0 / D_MODEL) + NORM_EPS)
        logits = br_ref[...]

        def normed(jj):
            cols = slice(jj * tn, (jj + 1) * tn)
            hn = (hacc_ref[jj] * rstd) * g2_ref[:, cols]
            return hn, jnp.dot(hn.astype(BF16), wr_ref[cols, :], preferred_element_type=F32)

        kb = FFN_KC // tn
        for wb in range(nj // 2):
            q, hh = divmod(wb, kb)
            lo, d_lo = normed(2 * q * kb + hh)
            hi, d_hi = normed((2 * q + 1) * kb + hh)
            logits = logits + d_lo + d_hi
            packed = (_bf16_bits(hi) & jnp.uint32(0xFFFF0000)) | (_bf16_bits(lo) >> 16)
            for c in range(tn // LANES):
                ct = wb * (tn // LANES) + c
                hp_ref[pl.ds(ct, tm, stride=ROW_TILES), :] = packed[:, c * LANES:(c + 1) * LANES]

        lane = lax.broadcasted_iota(I32, logits.shape, 1)
        neg = jnp.float32(-jnp.inf)
        big = jnp.int32(LANES)
        cmask = lane < N_GROUPS
        cl = jnp.where(cmask, logits, neg)
        cmax = jnp.max(cl, axis=-1, keepdims=True)
        g_sel = jnp.min(jnp.where(cl == cmax, lane, big), axis=-1, keepdims=True)
        p_sel = 1.0 / jnp.sum(jnp.where(cmask, jnp.exp(logits - cmax), 0.0), axis=-1, keepdims=True)
        f0 = N_GROUPS + g_sel * EXPERTS_PER_GROUP
        fl = jnp.where((lane >= f0) & (lane < f0 + EXPERTS_PER_GROUP), logits, neg)
        v1 = jnp.max(fl, axis=-1, keepdims=True)
        i1 = jnp.min(jnp.where(fl == v1, lane, big), axis=-1, keepdims=True)
        fl2 = jnp.where(lane == i1, neg, fl)
        v2 = jnp.max(fl2, axis=-1, keepdims=True)
        i2 = jnp.min(jnp.where(fl2 == v2, lane, big), axis=-1, keepdims=True)
        e2 = jnp.exp(v2 - v1)
        w1 = p_sel / (1.0 + e2)
        w2 = p_sel * e2 / (1.0 + e2)
        gate_ref[...] = jnp.where(lane == 0, w1, jnp.where(lane == 1, w2, 0.0))
        eid_ref[...] = jnp.where(lane == 0, i1 - N_GROUPS, jnp.where(lane == 1, i2 - N_GROUPS, 0))


def _outproj(ya, yb, x, wo, g2, wr, br, *, tm, tn):
    t = x.shape[0]
    nj = D_MODEL // tn
    assert FFN_KC % tn == 0 and GMLP_WIDTH == GLA_WIDTH
    return pl.pallas_call(
        functools.partial(_outproj_body, nj=nj, tn=tn),
        grid=(t // tm, nj),
        in_specs=[
            pl.BlockSpec((tm, GMLP_WIDTH), lambda i, j: (i, 0)),
            pl.BlockSpec((tm, GLA_WIDTH), lambda i, j: (i, 0)),
            pl.BlockSpec((tm, tn), lambda i, j: (i, j)),
            pl.BlockSpec((GMLP_WIDTH, tn), lambda i, j: (0, j)),
            pl.BlockSpec((GLA_WIDTH, tn), lambda i, j: (1, j)),
            pl.BlockSpec((1, D_MODEL), lambda i, j: (0, 0)),
            pl.BlockSpec((D_MODEL, LANES), lambda i, j: (0, 0)),
            pl.BlockSpec((1, LANES), lambda i, j: (0, 0)),
        ],
        out_specs=[
            pl.BlockSpec((tm, tn), lambda i, j: (i, j)),
            pl.BlockSpec((tm * ROW_TILES, LANES), lambda i, j: (i, 0)),
            pl.BlockSpec((tm, LANES), lambda i, j: (i, 0)),
            pl.BlockSpec((tm, LANES), lambda i, j: (i, 0)),
        ],
        out_shape=[
            jax.ShapeDtypeStruct((t, D_MODEL), F32),
            jax.ShapeDtypeStruct((t * ROW_TILES, LANES), U32),
            jax.ShapeDtypeStruct((t, LANES), F32),
            jax.ShapeDtypeStruct((t, LANES), I32),
        ],
        scratch_shapes=[pltpu.VMEM((nj, tm, tn), F32)],
        compiler_params=_params(("arbitrary", "arbitrary")),
        name="outproj_router",
    )(ya, yb, x, wo, wo, g2, wr, br)


FFN_UP_STEPS = D_MODEL // FFN_KC
FFN_NC = 1024
FFN_DOWN_STEPS = D_MODEL // FFN_NC
ROW_UNIT = 128
DMA_UNROLL = 8


def _ffn_body(sbe_ref, sbr_ref, nsb_ref, tok0_ref, tokn_ref, hp_hbm, wg_ref, wu_ref, wd_ref, ys_ref,
              g_ref, sem, xh_ref, gacc_ref, uacc_ref, a_ref, wgb_ref, wub_ref, wdb_ref, *, rows):
    s = pl.program_id(0)
    p = pl.program_id(1)
    nsb = nsb_ref[0]
    valid = s < nsb
    slot = lax.rem(s, 2)
    word_tiles = FFN_KC // LANES

    def units(n_rows):
        return lax.shift_right_logical(n_rows + (ROW_UNIT - 1), ROW_UNIT.bit_length() - 1)

    n_units = units(sbr_ref[s])

    def row_copy(tok_ref, slot_, i):
        src = hp_hbm.at[pl.ds(pl.multiple_of(tok_ref[0, 0, i] * ROW_TILES, ROW_TILES), ROW_TILES)]
        dst = g_ref.at[slot_, pl.ds(pl.multiple_of(i * ROW_TILES, ROW_TILES), ROW_TILES)]
        return pltpu.make_async_copy(src, dst, sem.at[slot_])

    def for_each_row(n_units_, fn):
        def unit_body(u, c):
            def body(i, c2):
                fn(u * ROW_UNIT + i)
                return c2
            return lax.fori_loop(0, ROW_UNIT, body, c, unroll=DMA_UNROLL)
        lax.fori_loop(0, n_units_, unit_body, 0)

    def for_row_blocks(fn):
        pair = 2 * ROW_UNIT
        for j in range(rows // pair):
            @pl.when(n_units >= 2 * j + 2)
            def _():
                fn(pair * j, pair)

            @pl.when(n_units == 2 * j + 1)
            def _():
                fn(pair * j, ROW_UNIT)
        if rows % pair:
            @pl.when(n_units == rows // ROW_UNIT)
            def _():
                fn(rows - ROW_UNIT, ROW_UNIT)

    @pl.when(p == 0)
    def _():
        @pl.when(s == 0)
        def _():
            for_each_row(units(sbr_ref[0]), lambda i: row_copy(tok0_ref, 0, i).start())

        @pl.when(valid)
        def _():
            for_each_row(n_units, lambda i: row_copy(tok0_ref, slot, i).wait())

        @pl.when(s + 1 < nsb)
        def _():
            nxt = jnp.minimum(s + 1, sbr_ref.shape[0] - 1)
            for_each_row(units(sbr_ref[nxt]), lambda i: row_copy(tokn_ref, 1 - slot, i).start())

    for step in range(FFN_UP_STEPS):
        @pl.when(valid & (p == step))
        def _(step=step):
            wgb_ref[...] = wg_ref[...].astype(BF16)
            wub_ref[...] = wu_ref[...].astype(BF16)

            def block(r0, m):
                r = slice(r0, r0 + m)
                if step % 2 == 0:
                    first = r0 * ROW_TILES + (step // 2) * word_tiles
                    w = jnp.concatenate(
                        [g_ref[slot, pl.ds(first + c, m, stride=ROW_TILES), :] for c in range(word_tiles)], axis=1)
                    xk = lax.bitcast_convert_type(w << 16, F32).astype(BF16)
                    xh_ref[r, :] = lax.bitcast_convert_type(w & jnp.uint32(0xFFFF0000), F32).astype(BF16)
                else:
                    xk = xh_ref[r, :]
                g = jnp.dot(xk, wgb_ref[...], preferred_element_type=F32)
                u = jnp.dot(xk, wub_ref[...], preferred_element_type=F32)
                if step > 0:
                    g = gacc_ref[r, :] + g
                    u = uacc_ref[r, :] + u
                if step < FFN_UP_STEPS - 1:
                    gacc_ref[r, :] = g
                    uacc_ref[r, :] = u
                else:
                    a_ref[r, :] = ((g * jax.nn.sigmoid(g)) * u).astype(BF16)

            for_row_blocks(block)

    @pl.when(p >= FFN_UP_STEPS)
    def _():
        @pl.when(valid)
        def _():
            wdb_ref[...] = wd_ref[...].astype(BF16)

        def block(r0, m):
            ys_ref[r0:r0 + m, :] = jnp.dot(a_ref[r0:r0 + m, :], wdb_ref[...], preferred_element_type=F32)

        for_row_blocks(block)
        for k in range(rows // ROW_UNIT):
            @pl.when(n_units <= k)
            def _(k=k):
                ys_ref[k * ROW_UNIT:(k + 1) * ROW_UNIT, :] = jnp.zeros((ROW_UNIT, ys_ref.shape[1]), F32)


def _ffn(sb_e, sb_rows, nsb, tok3, hp, w_gate, w_up, w_down, *, rows):
    s_max = sb_e.shape[0]
    assert rows % ROW_UNIT == 0
    n_steps = FFN_UP_STEPS + FFN_DOWN_STEPS
    last_up = FFN_UP_STEPS - 1
    last_down = FFN_DOWN_STEPS - 1

    def gate_idx(s, p, sbe, sbr, nsb):
        last_s = nsb[0] - 1
        ahead = (p == n_steps - 1) & (s < last_s)
        e = jnp.where(ahead, sbe[jnp.minimum(s + 1, last_s)], sbe[jnp.minimum(s, last_s)])
        k = jnp.where(ahead, 0, jnp.where(s <= last_s, jnp.minimum(p, last_up), last_up))
        return (e, k, 0)

    def up_idx(s, p, sbe, sbr, nsb):
        last_s = nsb[0] - 1
        return (sbe[jnp.minimum(s, last_s)], jnp.where(s <= last_s, jnp.minimum(p, last_up), last_up), 0)

    def down_idx(s, p, sbe, sbr, nsb):
        last_s = nsb[0] - 1
        prev = jnp.clip(s - 1, 0, last_s)
        in_down = (s <= last_s) & (p >= FFN_UP_STEPS)
        return (jnp.where(in_down, sbe[jnp.minimum(s, last_s)], sbe[prev]), 0,
                jnp.where(in_down, p - FFN_UP_STEPS, last_down))

    return pl.pallas_call(
        functools.partial(_ffn_body, rows=rows),
        grid_spec=pltpu.PrefetchScalarGridSpec(
            num_scalar_prefetch=3,
            grid=(s_max, n_steps),
            in_specs=[
                pl.BlockSpec((1, 1, rows), lambda s, p, sbe, sbr, nsb: (0, 0, 0), memory_space=pltpu.SMEM),
                pl.BlockSpec((1, 1, rows), lambda s, p, sbe, sbr, nsb: (jnp.minimum(s + 1, s_max - 1), 0, 0),
                             memory_space=pltpu.SMEM),
                pl.BlockSpec(memory_space=pl.ANY),
                pl.BlockSpec((None, FFN_KC, D_EXPERT), gate_idx),
                pl.BlockSpec((None, FFN_KC, D_EXPERT), up_idx),
                pl.BlockSpec((None, D_EXPERT, FFN_NC), down_idx),
            ],
            out_specs=pl.BlockSpec((rows, FFN_NC),
                                   lambda s, p, sbe, sbr, nsb: (s, jnp.maximum(p - FFN_UP_STEPS, 0))),
            scratch_shapes=[
                pltpu.VMEM((2, rows * ROW_TILES, LANES), U32),
                pltpu.SemaphoreType.DMA((2,)),
                pltpu.VMEM((rows, FFN_KC), BF16),
                pltpu.VMEM((rows, D_EXPERT), F32),
                pltpu.VMEM((rows, D_EXPERT), F32),
                pltpu.VMEM((rows, D_EXPERT), BF16),
                pltpu.VMEM((FFN_KC, D_EXPERT), BF16),
                pltpu.VMEM((FFN_KC, D_EXPERT), BF16),
                pltpu.VMEM((D_EXPERT, FFN_NC), BF16),
            ],
        ),
        out_shape=jax.ShapeDtypeStruct((s_max * rows, D_MODEL), F32),
        compiler_params=_params(("arbitrary", "arbitrary")),
        name="moe_ffn",
    )(sb_e, sb_rows, nsb, tok3, tok3, hp, w_gate, w_up, w_down)


def _combine_body(pos0_ref, posn_ref, h1_ref, gate_ref, gf_ref, ys_hbm, out_ref, buf_ref, sem, *, tb):
    i = pl.program_id(0)
    slot = lax.rem(i, 2)

    def row_copy(pos_ref, slot_, k, j):
        src = ys_hbm.at[pl.ds(pos_ref[0, k, j], 1)]
        return pltpu.make_async_copy(src, buf_ref.at[slot_, k, pl.ds(j, 1)], sem.at[slot_])

    def for_each_row(fn):
        def body(j, c):
            fn(0, j)
            fn(1, j)
            return c
        lax.fori_loop(0, tb, body, 0, unroll=DMA_UNROLL)

    @pl.when(i == 0)
    def _():
        for_each_row(lambda k, j: row_copy(pos0_ref, 0, k, j).start())

    for_each_row(lambda k, j: row_copy(pos0_ref, slot, k, j).wait())

    @pl.when(i + 1 < pl.num_programs(0))
    def _():
        for_each_row(lambda k, j: row_copy(posn_ref, 1 - slot, k, j).start())

    gate = gate_ref[...]
    y = buf_ref[slot, 0] * gate[:, 0:1] + buf_ref[slot, 1] * gate[:, 1:2]
    h = h1_ref[...] + y
    ms = jnp.mean(h * h, axis=-1, keepdims=True)
    out_ref[...] = (h * lax.rsqrt(ms + NORM_EPS)) * gf_ref[...]


def _combine(pos3, h1, gate, gf, ys, *, tb):
    t = h1.shape[0]
    n_blocks = t // tb
    return pl.pallas_call(
        functools.partial(_combine_body, tb=tb),
        grid=(n_blocks,),
        in_specs=[
            pl.BlockSpec((1, 2, tb), lambda i: (0, 0, 0), memory_space=pltpu.SMEM),
            pl.BlockSpec((1, 2, tb), lambda i: (jnp.minimum(i + 1, n_blocks - 1), 0, 0), memory_space=pltpu.SMEM),
            pl.BlockSpec((tb, D_MODEL), lambda i: (i, 0)),
            pl.BlockSpec((tb, LANES), lambda i: (i, 0)),
            pl.BlockSpec((1, D_MODEL), lambda i: (0, 0)),
            pl.BlockSpec(memory_space=pl.ANY),
        ],
        out_specs=pl.BlockSpec((tb, D_MODEL), lambda i: (i, 0)),
        out_shape=jax.ShapeDtypeStruct((t, D_MODEL), F32),
        scratch_shapes=[pltpu.VMEM((2, 2, tb, D_MODEL), F32), pltpu.SemaphoreType.DMA((2,))],
        compiler_params=_params(("arbitrary",)),
        name="moe_combine",
    )(pos3, pos3, h1, gate, gf, ys)


def _dispatch_plan(eid, *, rows, s_max):
    t = eid.shape[0]
    flat_e = eid[:, :2].reshape(-1)
    onehot = (flat_e[:, None] == jnp.arange(N_EXPERTS, dtype=I32)[None, :]).astype(I32)
    csum = jnp.cumsum(onehot, axis=0)
    rank = jnp.sum(csum * onehot, axis=1) - 1
    counts = csum[-1]
    nsb_e = (counts + rows - 1) // rows
    sb_end = jnp.cumsum(nsb_e)
    sb_start = sb_end - nsb_e
    nsb = sb_end[-1]
    pos = sb_start[flat_e] * rows + rank
    s_idx = jnp.minimum(jnp.arange(s_max, dtype=I32), nsb - 1)
    sb_e = jnp.minimum(jnp.searchsorted(sb_end, s_idx, side="right"), N_EXPERTS - 1).astype(I32)
    sb_rows = jnp.clip(counts[sb_e] - (s_idx - sb_start[sb_e]) * rows, 0, rows)
    sb_rows = jnp.where(jnp.arange(s_max) < nsb, sb_rows, 0).astype(I32)
    flat_tok = jnp.arange(2 * t, dtype=I32) // 2
    tok = jnp.zeros((s_max * rows,), I32).at[pos].set(flat_tok)
    return sb_e, sb_rows, nsb.reshape(1).astype(I32), tok.reshape(s_max, 1, rows), pos.reshape(t, 2)


def _layer(x, norm1_g, w_in, gmlp_v_g, gmlp_ws, gmlp_bs, gla_wa2, gla_ba, gla_norm_g, w_out, norm2_g,
           router_coarse_w, router_coarse_b, router_fine_w, router_fine_b, exp_w_gate, exp_w_up, exp_w_down,
           norm_f_g, *, tm_in, tb_gmlp, tb_gla, tm_out, tb_comb, sb_rows):
    t = x.shape[0]
    w_main = w_in.astype(BF16)
    w_lr = jnp.pad(w_in[:, D_PROJ_MAIN:], ((0, 0), (0, LANES - GLA_GATE_RANK))).astype(BF16)
    wa2p = jnp.pad(gla_wa2, ((0, LANES - GLA_GATE_RANK), (0, 0))).astype(BF16)
    wo = w_out.astype(BF16)
    n_route = N_GROUPS + N_EXPERTS
    wr = jnp.pad(jnp.concatenate([router_coarse_w, router_fine_w], axis=1),
                 ((0, 0), (0, LANES - n_route))).astype(BF16)
    br = jnp.pad(jnp.concatenate([router_coarse_b, router_fine_b]), (0, LANES - n_route)).reshape(1, LANES)

    z, lr = _inproj(x, norm1_g.reshape(1, -1), w_main, w_lr, tm=tm_in, tn=1024)
    y_a = _gmlp(z, gmlp_v_g.reshape(1, -1), gmlp_ws, gmlp_bs.T, tb=tb_gmlp)
    y_b = _gla(z, lr, wa2p, gla_ba.reshape(1, -1), gla_norm_g.reshape(1, -1), tb=tb_gla)
    h1, hp, gate, eid = _outproj(y_a, y_b, x, wo, norm2_g.reshape(1, -1), wr, br, tm=tm_out, tn=512)

    s_max = N_EXPERTS + (2 * t) // sb_rows
    sb_e, sb_n, nsb, tok3, pos = _dispatch_plan(eid, rows=sb_rows, s_max=s_max)
    ys = _ffn(sb_e, sb_n, nsb, tok3, hp, exp_w_gate, exp_w_up, exp_w_down, rows=sb_rows)
    pos3 = pos.reshape(t // tb_comb, tb_comb, 2).transpose(0, 2, 1)
    return _combine(pos3, h1, gate, norm_f_g.reshape(1, -1), ys, tb=tb_comb)


def kernel(x, norm1_g, w_in, gmlp_v_g, gmlp_ws, gmlp_bs, gla_wa2, gla_ba, gla_norm_g, w_out, norm2_g,
           router_coarse_w, router_coarse_b, router_fine_w, router_fine_b, exp_w_gate, exp_w_up, exp_w_down,
           norm_f_g):
    b, t, d = x.shape
    assert b == 1 and d == D_MODEL and norm1_g.shape[0] == 1, "one sequence, one layer, the stated widths"

    def first(a):
        return a.reshape(a.shape[1:])

    out = _layer(
        first(x), first(norm1_g), first(w_in), first(gmlp_v_g), first(gmlp_ws), first(gmlp_bs), first(gla_wa2),
        first(gla_ba), first(gla_norm_g), first(w_out), first(norm2_g), first(router_coarse_w),
        first(router_coarse_b), first(router_fine_w), first(router_fine_b), first(exp_w_gate), first(exp_w_up),
        first(exp_w_down), norm_f_g,
        tm_in=512, tb_gmlp=256, tb_gla=512, tm_out=512, tb_comb=256, sb_rows=640)
    return out.reshape(b, t, d)
```

```python
import functools

import jax
import jax.numpy as jnp
from jax import lax
from jax.experimental import pallas as pl
from jax.experimental.pallas import tpu as pltpu

F32 = jnp.float32
BF16 = jnp.bfloat16
U32 = jnp.uint32
I32 = jnp.int32

D_MODEL = 4096
GMLP_WIDTH = 2048
GMLP_HEADS = 8
GMLP_HEAD_DIM = GMLP_WIDTH // GMLP_HEADS
GMLP_CHUNK = 128
GLA_WIDTH = 2048
GLA_HEADS = 4
GLA_KEY_WIDTH = 1024
GLA_HEAD_DK = GLA_KEY_WIDTH // GLA_HEADS
GLA_HEAD_DV = GLA_WIDTH // GLA_HEADS
GLA_GATE_RANK = 16
GLA_GATE_NORMALIZER = 16.0
GLA_CHUNK = 64
N_GROUPS = 8
EXPERTS_PER_GROUP = 8
N_EXPERTS = N_GROUPS * EXPERTS_PER_GROUP
D_EXPERT = D_MODEL // 4
NORM_EPS = 1e-6
D_PROJ_MAIN = 2 * GMLP_WIDTH + 2 * GLA_KEY_WIDTH + 2 * GLA_WIDTH

LANES = 128
HALF = D_MODEL // 2
ROW_TILES = HALF // LANES
FFN_KC = 1024
VMEM_LIMIT = 58 * 1024 * 1024

HIGHEST = lax.Precision.HIGHEST


def _params(semantics, vmem=VMEM_LIMIT):
    return pltpu.CompilerParams(dimension_semantics=semantics, vmem_limit_bytes=vmem)


def _inproj_body(x_ref, g_ref, w_ref, wlr_ref, z_ref, lr_ref, xn_ref):
    @pl.when(pl.program_id(1) == 0)
    def _():
        x = x_ref[...]
        ms = jnp.mean(x * x, axis=-1, keepdims=True)
        xn = ((x * lax.rsqrt(ms + NORM_EPS)) * g_ref[...]).astype(BF16)
        xn_ref[...] = xn
        lr_ref[...] = jnp.dot(xn, wlr_ref[...], preferred_element_type=F32)

    z_ref[...] = jnp.dot(xn_ref[...], w_ref[...], preferred_element_type=F32)


def _inproj(x, g, w_main, w_lr, *, tm, tn):
    t = x.shape[0]
    return pl.pallas_call(
        _inproj_body,
        grid=(t // tm, D_PROJ_MAIN // tn),
        in_specs=[
            pl.BlockSpec((tm, D_MODEL), lambda i, j: (i, 0)),
            pl.BlockSpec((1, D_MODEL), lambda i, j: (0, 0)),
            pl.BlockSpec((D_MODEL, tn), lambda i, j: (0, j)),
            pl.BlockSpec((D_MODEL, LANES), lambda i, j: (0, 0)),
        ],
        out_specs=[
            pl.BlockSpec((tm, tn), lambda i, j: (i, j)),
            pl.BlockSpec((tm, LANES), lambda i, j: (i, 0)),
        ],
        out_shape=[
            jax.ShapeDtypeStruct((t, D_PROJ_MAIN), F32),
            jax.ShapeDtypeStruct((t, LANES), F32),
        ],
        scratch_shapes=[pltpu.VMEM((tm, D_MODEL), BF16)],
        compiler_params=_params(("arbitrary", "arbitrary")),
        name="inproj",
    )(x, g, w_main, w_lr)


def _gelu(x):
    return x * (lax.erf(x * (2.0 ** -0.5)) + 1.0) * 0.5


def _gmlp_body(u_ref, v_ref, vg_ref, ws_ref, bst_ref, y_ref, *, n_chunks):
    row = lax.broadcasted_iota(I32, (GMLP_CHUNK, GMLP_CHUNK), 0)
    col = lax.broadcasted_iota(I32, (GMLP_CHUNK, GMLP_CHUNK), 1)
    causal = row >= col
    for c in range(n_chunks):
        rows = slice(c * GMLP_CHUNK, (c + 1) * GMLP_CHUNK)
        gv = _gelu(v_ref[rows, :])
        mu = jnp.mean(gv, axis=-1, keepdims=True)
        vc = gv - mu
        var = jnp.mean(vc * vc, axis=-1, keepdims=True)
        vn = ((vc * lax.rsqrt(var + NORM_EPS)) * vg_ref[...]).astype(BF16)
        for h in range(GMLP_HEADS):
            cols = slice(h * GMLP_HEAD_DIM, (h + 1) * GMLP_HEAD_DIM)
            wm = jnp.where(causal, ws_ref[h], 0.0).astype(BF16)
            s = jnp.dot(wm, vn[:, cols], preferred_element_type=F32) + bst_ref[:, h:h + 1]
            y_ref[rows, cols] = (_gelu(u_ref[rows, cols]) * s).astype(BF16)


def _gmlp(z, v_g, ws, bs_t, *, tb):
    t = z.shape[0]
    blk = GMLP_WIDTH
    return pl.pallas_call(
        functools.partial(_gmlp_body, n_chunks=tb // GMLP_CHUNK),
        grid=(t // tb,),
        in_specs=[
            pl.BlockSpec((tb, blk), lambda i: (i, 0)),
            pl.BlockSpec((tb, blk), lambda i: (i, 1)),
            pl.BlockSpec((1, blk), lambda i: (0, 0)),
            pl.BlockSpec((GMLP_HEADS, GMLP_CHUNK, GMLP_CHUNK), lambda i: (0, 0, 0)),
            pl.BlockSpec((GMLP_CHUNK, GMLP_HEADS), lambda i: (0, 0)),
        ],
        out_specs=pl.BlockSpec((tb, blk), lambda i: (i, 0)),
        out_shape=jax.ShapeDtypeStruct((t, blk), BF16),
        compiler_params=_params(("arbitrary",)),
        name="gmlp",
    )(z, z, v_g, ws, bs_t)


def _gla_body(q_ref, k_ref, v_ref, r_ref, lr_ref, wa2_ref, ba_ref, gn_ref, y_ref, st_ref, bcum_ref, *, n_chunks):
    @pl.when(pl.program_id(0) == 0)
    def _():
        st_ref[...] = jnp.zeros_like(st_ref)

    c_len = GLA_CHUNK
    tb = lr_ref.shape[0]
    tn_dims = (((0,), (0,)), ((), ()))
    nt_dims = (((1,), (1,)), ((), ()))

    logit = jnp.dot(lr_ref[...].astype(BF16), wa2_ref[...], preferred_element_type=F32) + ba_ref[...]
    log_a = (jnp.minimum(logit, 0.0) - jnp.log1p(jnp.exp(-jnp.abs(logit)))) * (1.0 / GLA_GATE_NORMALIZER)
    row_b = lax.broadcasted_iota(I32, (tb, tb), 0)
    col_b = lax.broadcasted_iota(I32, (tb, tb), 1)
    shift = c_len.bit_length() - 1
    same_chunk = lax.shift_right_logical(row_b, shift) == lax.shift_right_logical(col_b, shift)
    lmat = jnp.where(same_chunk & (row_b >= col_b), 1.0, 0.0).astype(F32)
    bcum_ref[...] = jnp.dot(lmat, log_a, precision=HIGHEST, preferred_element_type=F32)

    row = lax.broadcasted_iota(I32, (c_len, c_len), 0)
    col = lax.broadcasted_iota(I32, (c_len, c_len), 1)
    causal = row >= col

    def chunk(c, carry):
        t0 = pl.multiple_of(c * c_len, c_len)
        rows = pl.ds(t0, c_len)
        for h in range(GLA_HEADS):
            kc = slice(h * GLA_HEAD_DK, (h + 1) * GLA_HEAD_DK)
            vc = slice(h * GLA_HEAD_DV, (h + 1) * GLA_HEAD_DV)
            bcum = bcum_ref[rows, kc]
            b_last = bcum[c_len - 1:c_len, :]
            q = q_ref[rows, kc] * (GLA_HEAD_DK ** -0.5)
            k = k_ref[rows, kc]
            q_e = (q * jnp.exp(bcum)).astype(BF16)
            k_e = (k * jnp.exp(-bcum)).astype(BF16)
            k_end = (k * jnp.exp(b_last - bcum)).astype(BF16)
            v = v_ref[rows, vc].astype(BF16)
            scores = lax.dot_general(q_e, k_e, nt_dims, preferred_element_type=F32)
            scores = jnp.where(causal, scores, 0.0).astype(BF16)
            state_t = st_ref[h]
            o = jnp.dot(scores, v, preferred_element_type=F32)
            o = o + lax.dot_general(q_e, state_t.astype(BF16), nt_dims, preferred_element_type=F32)
            kv_t = lax.dot_general(v, k_end, tn_dims, preferred_element_type=F32)
            st_ref[h] = state_t * jnp.exp(b_last) + kv_t
            o = (o * lax.rsqrt(jnp.mean(o * o, axis=-1, keepdims=True) + NORM_EPS)) * gn_ref[...]
            r = r_ref[rows, vc]
            y_ref[rows, vc] = (o * (r * jax.nn.sigmoid(r))).astype(BF16)
        return carry

    lax.fori_loop(0, n_chunks, chunk, 0)


def _gla(z, lr, wa2p, ba, gn, *, tb):
    t = z.shape[0]
    kw, vw = GLA_KEY_WIDTH, GLA_WIDTH
    return pl.pallas_call(
        functools.partial(_gla_body, n_chunks=tb // GLA_CHUNK),
        grid=(t // tb,),
        in_specs=[
            pl.BlockSpec((tb, kw), lambda i: (i, 4)),
            pl.BlockSpec((tb, kw), lambda i: (i, 5)),
            pl.BlockSpec((tb, vw), lambda i: (i, 3)),
            pl.BlockSpec((tb, vw), lambda i: (i, 4)),
            pl.BlockSpec((tb, LANES), lambda i: (i, 0)),
            pl.BlockSpec((LANES, kw), lambda i: (0, 0)),
            pl.BlockSpec((1, kw), lambda i: (0, 0)),
            pl.BlockSpec((1, GLA_HEAD_DV), lambda i: (0, 0)),
        ],
        out_specs=pl.BlockSpec((tb, vw), lambda i: (i, 0)),
        out_shape=jax.ShapeDtypeStruct((t, vw), BF16),
        scratch_shapes=[pltpu.VMEM((GLA_HEADS, GLA_HEAD_DV, GLA_HEAD_DK), F32),
                        pltpu.VMEM((tb, kw), F32)],
        compiler_params=_params(("arbitrary",)),
        name="gla",
    )(z, z, z, z, lr, wa2p, ba, gn)


def _bf16_bits(x):
    return lax.bitcast_convert_type(x.astype(BF16).astype(F32), U32)


def _outproj_body(ya_ref, yb_ref, x_ref, woa_ref, wob_ref, g2_ref, wr_ref, br_ref,
                  h1_ref, hp_ref, gate_ref, eid_ref, hacc_ref, *, nj, tn):
    j = pl.program_id(1)
    tm = x_ref.shape[0]
    acc = jnp.dot(ya_ref[...], woa_ref[...], preferred_element_type=F32)
    acc = acc + jnp.dot(yb_ref[...], wob_ref[...], preferred_element_type=F32)
    h = x_ref[...] + acc
    h1_ref[...] = h
    hacc_ref[j] = h

    @pl.when(j == nj - 1)
    def _():
        ssq = jnp.sum(hacc_ref[0] * hacc_ref[0], axis=-1, keepdims=True)
        for jj in range(1, nj):
            ssq = ssq + jnp.sum(hacc_ref[jj] * hacc_ref[jj], axis=-1, keepdims=True)
        rstd = lax.rsqrt(ssq * (1.0 / D_MODEL) + NORM_EPS)
        logits = br_ref[...]

        def normed(jj):
            cols = slice(jj * tn, (jj + 1) * tn)
            hn = (hacc_ref[jj] * rstd) * g2_ref[:, cols]
            return hn, jnp.dot(hn.astype(BF16), wr_ref[cols, :], preferred_element_type=F32)

        kb = FFN_KC // tn
        for wb in range(nj // 2):
            q, hh = divmod(wb, kb)
            lo, d_lo = normed(2 * q * kb + hh)
            hi, d_hi = normed((2 * q + 1) * kb + hh)
            logits = logits + d_lo + d_hi
            packed = (_bf16_bits(hi) & jnp.uint32(0xFFFF0000)) | (_bf16_bits(lo) >> 16)
            for c in range(tn // LANES):
                ct = wb * (tn // LANES) + c
                hp_ref[pl.ds(ct, tm, stride=ROW_TILES), :] = packed[:, c * LANES:(c + 1) * LANES]

        lane = lax.broadcasted_iota(I32, logits.shape, 1)
        neg = jnp.float32(-jnp.inf)
        big = jnp.int32(LANES)
        cmask = lane < N_GROUPS
        cl = jnp.where(cmask, logits, neg)
        cmax = jnp.max(cl, axis=-1, keepdims=True)
        g_sel = jnp.min(jnp.where(cl == cmax, lane, big), axis=-1, keepdims=True)
        p_sel = 1.0 / jnp.sum(jnp.where(cmask, jnp.exp(logits - cmax), 0.0), axis=-1, keepdims=True)
        f0 = N_GROUPS + g_sel * EXPERTS_PER_GROUP
        fl = jnp.where((lane >= f0) & (lane < f0 + EXPERTS_PER_GROUP), logits, neg)
        v1 = jnp.max(fl, axis=-1, keepdims=True)
        i1 = jnp.min(jnp.where(fl == v1, lane, big), axis=-1, keepdims=True)
        fl2 = jnp.where(lane == i1, neg, fl)
        v2 = jnp.max(fl2, axis=-1, keepdims=True)
        i2 = jnp.min(jnp.where(fl2 == v2, lane, big), axis=-1, keepdims=True)
        e2 = jnp.exp(v2 - v1)
        w1 = p_sel / (1.0 + e2)
        w2 = p_sel * e2 / (1.0 + e2)
        gate_ref[...] = jnp.where(lane == 0, w1, jnp.where(lane == 1, w2, 0.0))
        eid_ref[...] = jnp.where(lane == 0, i1 - N_GROUPS, jnp.where(lane == 1, i2 - N_GROUPS, 0))


def _outproj(ya, yb, x, wo, g2, wr, br, *, tm, tn):
    t = x.shape[0]
    nj = D_MODEL // tn
    assert FFN_KC % tn == 0 and GMLP_WIDTH == GLA_WIDTH
    return pl.pallas_call(
        functools.partial(_outproj_body, nj=nj, tn=tn),
        grid=(t // tm, nj),
        in_specs=[
            pl.BlockSpec((tm, GMLP_WIDTH), lambda i, j: (i, 0)),
            pl.BlockSpec((tm, GLA_WIDTH), lambda i, j: (i, 0)),
            pl.BlockSpec((tm, tn), lambda i, j: (i, j)),
            pl.BlockSpec((GMLP_WIDTH, tn), lambda i, j: (0, j)),
            pl.BlockSpec((GLA_WIDTH, tn), lambda i, j: (1, j)),
            pl.BlockSpec((1, D_MODEL), lambda i, j: (0, 0)),
            pl.BlockSpec((D_MODEL, LANES), lambda i, j: (0, 0)),
            pl.BlockSpec((1, LANES), lambda i, j: (0, 0)),
        ],
        out_specs=[
            pl.BlockSpec((tm, tn), lambda i, j: (i, j)),
            pl.BlockSpec((tm * ROW_TILES, LANES), lambda i, j: (i, 0)),
            pl.BlockSpec((tm, LANES), lambda i, j: (i, 0)),
            pl.BlockSpec((tm, LANES), lambda i, j: (i, 0)),
        ],
        out_shape=[
            jax.ShapeDtypeStruct((t, D_MODEL), F32),
            jax.ShapeDtypeStruct((t * ROW_TILES, LANES), U32),
            jax.ShapeDtypeStruct((t, LANES), F32),
            jax.ShapeDtypeStruct((t, LANES), I32),
        ],
        scratch_shapes=[pltpu.VMEM((nj, tm, tn), F32)],
        compiler_params=_params(("arbitrary", "arbitrary")),
        name="outproj_router",
    )(ya, yb, x, wo, wo, g2, wr, br)


FFN_UP_CHUNKS = D_MODEL // FFN_KC
FFN_NC = 1024
FFN_DOWN_CHUNKS = D_MODEL // FFN_NC
FFN_CHUNKS = 2 * FFN_UP_CHUNKS + FFN_DOWN_CHUNKS
FFN_NBUF = 3
ROW_UNIT = 128
DMA_UNROLL = 8
assert FFN_CHUNKS % FFN_NBUF == 0 and FFN_KC == D_EXPERT == FFN_NC


def _pack_bf16_pair(lo, hi):
    return (_bf16_bits(hi) & jnp.uint32(0xFFFF0000)) | (_bf16_bits(lo) >> 16)


def _unpack_lo(w):
    return lax.bitcast_convert_type(w << 16, F32)


def _unpack_hi(w):
    return lax.bitcast_convert_type(w & jnp.uint32(0xFFFF0000), F32)


def _ffn_body(sbe_ref, sbr_ref, nsb_ref, tok0_ref, tokn_ref, hp_hbm, wg_hbm, wu_hbm, wd_hbm, ys_ref,
              g_ref, gsem, wbuf_ref, wsem, wb_ref, xk_ref, gacc_ref, uacc_ref, a_ref, *, rows):
    s = pl.program_id(0)
    nsb = nsb_ref[0]
    valid = s < nsb
    slot = lax.rem(s, 2)
    s_next = jnp.minimum(s + 1, sbr_ref.shape[0] - 1)
    word_tiles = FFN_KC // LANES

    def units(n_rows):
        return lax.shift_right_logical(n_rows + (ROW_UNIT - 1), ROW_UNIT.bit_length() - 1)

    n_units = units(sbr_ref[s])

    def row_copy(tok_ref, slot_, i):
        src = hp_hbm.at[pl.ds(pl.multiple_of(tok_ref[0, 0, i] * ROW_TILES, ROW_TILES), ROW_TILES)]
        dst = g_ref.at[slot_, pl.ds(pl.multiple_of(i * ROW_TILES, ROW_TILES), ROW_TILES)]
        return pltpu.make_async_copy(src, dst, gsem.at[slot_])

    def for_each_row(n_units_, fn):
        def unit_body(u, c):
            def body(i, c2):
                fn(u * ROW_UNIT + i)
                return c2
            return lax.fori_loop(0, ROW_UNIT, body, c, unroll=DMA_UNROLL)
        lax.fori_loop(0, n_units_, unit_body, 0)

    def for_row_blocks(fn):
        pair = 2 * ROW_UNIT
        for j in range(rows // pair):
            @pl.when(n_units >= 2 * j + 2)
            def _():
                fn(pair * j, pair)

            @pl.when(n_units == 2 * j + 1)
            def _():
                fn(pair * j, ROW_UNIT)
        if rows % pair:
            @pl.when(n_units == rows // ROW_UNIT)
            def _():
                fn(rows - ROW_UNIT, ROW_UNIT)

    def chunk_copy(e, ci):
        if ci < 2 * FFN_UP_CHUNKS:
            w_hbm = wg_hbm if ci % 2 == 0 else wu_hbm
            src = w_hbm.at[e, pl.ds((ci // 2) * FFN_KC, FFN_KC), :]
        else:
            src = wd_hbm.at[e, :, pl.ds((ci - 2 * FFN_UP_CHUNKS) * FFN_NC, FFN_NC)]
        b = ci % FFN_NBUF
        return pltpu.make_async_copy(src, wbuf_ref.at[b], wsem.at[b])

    @pl.when(s == 0)
    def _():
        for ci in range(FFN_NBUF):
            chunk_copy(sbe_ref[0], ci).start()
        for_each_row(units(sbr_ref[0]), lambda i: row_copy(tok0_ref, 0, i).start())

    @pl.when(valid)
    def _():
        e_cur = sbe_ref[s]
        has_next = s + 1 < nsb
        for_each_row(n_units, lambda i: row_copy(tok0_ref, slot, i).wait())

        @pl.when(has_next)
        def _():
            for_each_row(units(sbr_ref[s_next]), lambda i: row_copy(tokn_ref, 1 - slot, i).start())

        for ci in range(FFN_CHUNKS):
            chunk_copy(e_cur, ci).wait()
            wb = wb_ref.at[ci % 2]
            wb[...] = wbuf_ref[ci % FFN_NBUF].astype(BF16)
            ahead = ci + FFN_NBUF
            if ahead < FFN_CHUNKS:
                chunk_copy(e_cur, ahead).start()
            else:
                @pl.when(has_next)
                def _(ahead=ahead):
                    chunk_copy(sbe_ref[s_next], ahead - FFN_CHUNKS).start()

            if ci < 2 * FFN_UP_CHUNKS:
                k, is_up = divmod(ci, 2)

                def block(r0, m, k=k, is_up=is_up, wb=wb):
                    r = slice(r0, r0 + m)
                    if not is_up and k % 2 == 0:
                        first = r0 * ROW_TILES + (k // 2) * word_tiles
                        w = jnp.concatenate(
                            [g_ref[slot, pl.ds(first + c, m, stride=ROW_TILES), :] for c in range(word_tiles)],
                            axis=1)
                        xk_ref[0, r, :] = _unpack_lo(w).astype(BF16)
                        xk_ref[1, r, :] = _unpack_hi(w).astype(BF16)
                    acc_ref = uacc_ref if is_up else gacc_ref
                    d = jnp.dot(xk_ref[k % 2, r, :], wb[...], preferred_element_type=F32)
                    if k > 0:
                        d = acc_ref[r, :] + d
                    if is_up and k == FFN_UP_CHUNKS - 1:
                        g = gacc_ref[r, :]
                        a_ref[r, :] = ((g * jax.nn.sigmoid(g)) * d).astype(BF16)
                    else:
                        acc_ref[r, :] = d
            else:
                n = ci - 2 * FFN_UP_CHUNKS

                def block(r0, m, n=n, wb=wb):
                    y = jnp.dot(a_ref[r0:r0 + m, :], wb[...], preferred_element_type=F32)
                    half = FFN_NC // 2
                    ys_ref[r0:r0 + m, n * half:(n + 1) * half] = _pack_bf16_pair(y[:, :half], y[:, half:])

            for_row_blocks(block)

        for u in range(rows // ROW_UNIT):
            @pl.when(n_units <= u)
            def _(u=u):
                ys_ref[u * ROW_UNIT:(u + 1) * ROW_UNIT, :] = jnp.zeros((ROW_UNIT, ys_ref.shape[1]), U32)

    @pl.when(jnp.logical_not(valid))
    def _():
        ys_ref[...] = jnp.zeros_like(ys_ref)


def _ffn(sb_e, sb_rows, nsb, tok3, hp, w_gate, w_up, w_down, *, rows):
    s_max = sb_e.shape[0]
    assert rows % ROW_UNIT == 0
    return pl.pallas_call(
        functools.partial(_ffn_body, rows=rows),
        grid_spec=pltpu.PrefetchScalarGridSpec(
            num_scalar_prefetch=3,
            grid=(s_max,),
            in_specs=[
                pl.BlockSpec((1, 1, rows), lambda s, sbe, sbr, nsb: (0, 0, 0), memory_space=pltpu.SMEM),
                pl.BlockSpec((1, 1, rows), lambda s, sbe, sbr, nsb: (jnp.minimum(s + 1, s_max - 1), 0, 0),
                             memory_space=pltpu.SMEM),
                pl.BlockSpec(memory_space=pl.ANY),
                pl.BlockSpec(memory_space=pl.ANY),
                pl.BlockSpec(memory_space=pl.ANY),
                pl.BlockSpec(memory_space=pl.ANY),
            ],
            out_specs=pl.BlockSpec((rows, HALF), lambda s, sbe, sbr, nsb: (s, 0)),
            scratch_shapes=[
                pltpu.VMEM((2, rows * ROW_TILES, LANES), U32),
                pltpu.SemaphoreType.DMA((2,)),
                pltpu.VMEM((FFN_NBUF, FFN_KC, D_EXPERT), F32),
                pltpu.SemaphoreType.DMA((FFN_NBUF,)),
                pltpu.VMEM((2, FFN_KC, D_EXPERT), BF16),
                pltpu.VMEM((2, rows, FFN_KC), BF16),
                pltpu.VMEM((rows, D_EXPERT), F32),
                pltpu.VMEM((rows, D_EXPERT), F32),
                pltpu.VMEM((rows, D_EXPERT), BF16),
            ],
        ),
        out_shape=jax.ShapeDtypeStruct((s_max * rows, HALF), U32),
        compiler_params=_params(("arbitrary",)),
        name="moe_ffn",
    )(sb_e, sb_rows, nsb, tok3, tok3, hp, w_gate, w_up, w_down)


def _combine_body(pos0_ref, posn_ref, h1_ref, gate_ref, gf_ref, ys_hbm, out_ref, buf_ref, sem, *, tb):
    i = pl.program_id(0)
    slot = lax.rem(i, 2)

    def row_copy(pos_ref, slot_, k, j):
        src = ys_hbm.at[pl.ds(pos_ref[0, k, j], 1)]
        return pltpu.make_async_copy(src, buf_ref.at[slot_, k, pl.ds(j, 1)], sem.at[slot_])

    def for_each_row(fn):
        def body(j, c):
            fn(0, j)
            fn(1, j)
            return c
        lax.fori_loop(0, tb, body, 0, unroll=DMA_UNROLL)

    @pl.when(i == 0)
    def _():
        for_each_row(lambda k, j: row_copy(pos0_ref, 0, k, j).start())

    for_each_row(lambda k, j: row_copy(pos0_ref, slot, k, j).wait())

    @pl.when(i + 1 < pl.num_programs(0))
    def _():
        for_each_row(lambda k, j: row_copy(posn_ref, 1 - slot, k, j).start())

    gate = gate_ref[...]
    half = FFN_NC // 2
    for n in range(FFN_DOWN_CHUNKS):
        w0 = buf_ref[slot, 0, :, n * half:(n + 1) * half]
        w1 = buf_ref[slot, 1, :, n * half:(n + 1) * half]
        for unpack, c0 in ((_unpack_lo, n * FFN_NC), (_unpack_hi, n * FFN_NC + half)):
            cols = slice(c0, c0 + half)
            y = unpack(w0) * gate[:, 0:1] + unpack(w1) * gate[:, 1:2]
            out_ref[:, cols] = h1_ref[:, cols] + y
    h = out_ref[...]
    ms = jnp.mean(h * h, axis=-1, keepdims=True)
    out_ref[...] = (h * lax.rsqrt(ms + NORM_EPS)) * gf_ref[...]


def _combine(pos3, h1, gate, gf, ys, *, tb):
    t = h1.shape[0]
    n_blocks = t // tb
    return pl.pallas_call(
        functools.partial(_combine_body, tb=tb),
        grid=(n_blocks,),
        in_specs=[
            pl.BlockSpec((1, 2, tb), lambda i: (0, 0, 0), memory_space=pltpu.SMEM),
            pl.BlockSpec((1, 2, tb), lambda i: (jnp.minimum(i + 1, n_blocks - 1), 0, 0), memory_space=pltpu.SMEM),
            pl.BlockSpec((tb, D_MODEL), lambda i: (i, 0)),
            pl.BlockSpec((tb, LANES), lambda i: (i, 0)),
            pl.BlockSpec((1, D_MODEL), lambda i: (0, 0)),
            pl.BlockSpec(memory_space=pl.ANY),
        ],
        out_specs=pl.BlockSpec((tb, D_MODEL), lambda i: (i, 0)),
        out_shape=jax.ShapeDtypeStruct((t, D_MODEL), F32),
        scratch_shapes=[pltpu.VMEM((2, 2, tb, HALF), U32), pltpu.SemaphoreType.DMA((2,))],
        compiler_params=_params(("arbitrary",)),
        name="moe_combine",
    )(pos3, pos3, h1, gate, gf, ys)


def _dispatch_plan(eid, *, rows, s_max):
    t = eid.shape[0]
    flat_e = eid[:, :2].reshape(-1)
    onehot = (flat_e[:, None] == jnp.arange(N_EXPERTS, dtype=I32)[None, :]).astype(I32)
    csum = jnp.cumsum(onehot, axis=0)
    rank = jnp.sum(csum * onehot, axis=1) - 1
    counts = csum[-1]
    nsb_e = (counts + rows - 1) // rows
    sb_end = jnp.cumsum(nsb_e)
    sb_start = sb_end - nsb_e
    nsb = sb_end[-1]
    pos = sb_start[flat_e] * rows + rank
    s_idx = jnp.minimum(jnp.arange(s_max, dtype=I32), nsb - 1)
    sb_e = jnp.minimum(jnp.searchsorted(sb_end, s_idx, side="right"), N_EXPERTS - 1).astype(I32)
    sb_rows = jnp.clip(counts[sb_e] - (s_idx - sb_start[sb_e]) * rows, 0, rows)
    sb_rows = jnp.where(jnp.arange(s_max) < nsb, sb_rows, 0).astype(I32)
    flat_tok = jnp.arange(2 * t, dtype=I32) // 2
    tok = jnp.zeros((s_max * rows,), I32).at[pos].set(flat_tok)
    return sb_e, sb_rows, nsb.reshape(1).astype(I32), tok.reshape(s_max, 1, rows), pos.reshape(t, 2)


def _layer(x, norm1_g, w_in, gmlp_v_g, gmlp_ws, gmlp_bs, gla_wa2, gla_ba, gla_norm_g, w_out, norm2_g,
           router_coarse_w, router_coarse_b, router_fine_w, router_fine_b, exp_w_gate, exp_w_up, exp_w_down,
           norm_f_g, *, tm_in, tb_gmlp, tb_gla, tm_out, tb_comb, sb_rows):
    t = x.shape[0]
    w_main = w_in.astype(BF16)
    w_lr = jnp.pad(w_in[:, D_PROJ_MAIN:], ((0, 0), (0, LANES - GLA_GATE_RANK))).astype(BF16)
    wa2p = jnp.pad(gla_wa2, ((0, LANES - GLA_GATE_RANK), (0, 0))).astype(BF16)
    wo = w_out.astype(BF16)
    n_route = N_GROUPS + N_EXPERTS
    wr = jnp.pad(jnp.concatenate([router_coarse_w, router_fine_w], axis=1),
                 ((0, 0), (0, LANES - n_route))).astype(BF16)
    br = jnp.pad(jnp.concatenate([router_coarse_b, router_fine_b]), (0, LANES - n_route)).reshape(1, LANES)

    z, lr = _inproj(x, norm1_g.reshape(1, -1), w_main, w_lr, tm=tm_in, tn=1024)
    y_a = _gmlp(z, gmlp_v_g.reshape(1, -1), gmlp_ws, gmlp_bs.T, tb=tb_gmlp)
    y_b = _gla(z, lr, wa2p, gla_ba.reshape(1, -1), gla_norm_g.reshape(1, -1), tb=tb_gla)
    h1, hp, gate, eid = _outproj(y_a, y_b, x, wo, norm2_g.reshape(1, -1), wr, br, tm=tm_out, tn=512)

    s_max = N_EXPERTS + (2 * t) // sb_rows
    sb_e, sb_n, nsb, tok3, pos = _dispatch_plan(eid, rows=sb_rows, s_max=s_max)
    ys = _ffn(sb_e, sb_n, nsb, tok3, hp, exp_w_gate, exp_w_up, exp_w_down, rows=sb_rows)
    pos3 = pos.reshape(t // tb_comb, tb_comb, 2).transpose(0, 2, 1)
    return _combine(pos3, h1, gate, norm_f_g.reshape(1, -1), ys, tb=tb_comb)


def kernel(x, norm1_g, w_in, gmlp_v_g, gmlp_ws, gmlp_bs, gla_wa2, gla_ba, gla_norm_g, w_out, norm2_g,
           router_coarse_w, router_coarse_b, router_fine_w, router_fine_b, exp_w_gate, exp_w_up, exp_w_down,
           norm_f_g):
    b, t, d = x.shape
    assert b == 1 and d == D_MODEL and norm1_g.shape[0] == 1, "one sequence, one layer, the stated widths"

    def first(a):
        return a.reshape(a.shape[1:])

    out = _layer(
        first(x), first(norm1_g), first(w_in), first(gmlp_v_g), first(gmlp_ws), first(gmlp_bs), first(gla_wa2),
        first(gla_ba), first(gla_norm_g), first(w_out), first(norm2_g), first(router_coarse_w),
        first(router_coarse_b), first(router_fine_w), first(router_fine_b), first(exp_w_gate), first(exp_w_up),
        first(exp_w_down), norm_f_g,
        tm_in=512, tb_gmlp=256, tb_gla=512, tm_out=512, tb_comb=512, sb_rows=768)
    return out.reshape(b, t, d)
```

```python
import functools

import jax
import jax.numpy as jnp
from jax import lax
from jax.experimental import pallas as pl
from jax.experimental.pallas import tpu as pltpu

F32 = jnp.float32
BF16 = jnp.bfloat16
U32 = jnp.uint32
I32 = jnp.int32

D_MODEL = 4096
GMLP_WIDTH = 2048
GMLP_HEADS = 8
GMLP_HEAD_DIM = GMLP_WIDTH // GMLP_HEADS
GMLP_CHUNK = 128
GLA_WIDTH = 2048
GLA_HEADS = 4
GLA_KEY_WIDTH = 1024
GLA_HEAD_DK = GLA_KEY_WIDTH // GLA_HEADS
GLA_HEAD_DV = GLA_WIDTH // GLA_HEADS
GLA_GATE_RANK = 16
GLA_GATE_NORMALIZER = 16.0
GLA_CHUNK = 64
N_GROUPS = 8
EXPERTS_PER_GROUP = 8
N_EXPERTS = N_GROUPS * EXPERTS_PER_GROUP
D_EXPERT = D_MODEL // 4
NORM_EPS = 1e-6
D_PROJ_MAIN = 2 * GMLP_WIDTH + 2 * GLA_KEY_WIDTH + 2 * GLA_WIDTH

LANES = 128
HALF = D_MODEL // 2
ROW_TILES = HALF // LANES
FFN_KC = 1024
VMEM_LIMIT = 58 * 1024 * 1024

HIGHEST = lax.Precision.HIGHEST


def _params(semantics, vmem=VMEM_LIMIT):
    return pltpu.CompilerParams(dimension_semantics=semantics, vmem_limit_bytes=vmem)


def _inproj_body(x_ref, g_ref, w_ref, wlr_ref, z_ref, lr_ref, xn_ref):
    @pl.when(pl.program_id(1) == 0)
    def _():
        x = x_ref[...]
        ms = jnp.mean(x * x, axis=-1, keepdims=True)
        xn = ((x * lax.rsqrt(ms + NORM_EPS)) * g_ref[...]).astype(BF16)
        xn_ref[...] = xn
        lr_ref[...] = jnp.dot(xn, wlr_ref[...], preferred_element_type=F32)

    z_ref[...] = jnp.dot(xn_ref[...], w_ref[...], preferred_element_type=F32)


def _inproj(x, g, w_main, w_lr, *, tm, tn):
    t = x.shape[0]
    return pl.pallas_call(
        _inproj_body,
        grid=(t // tm, D_PROJ_MAIN // tn),
        in_specs=[
            pl.BlockSpec((tm, D_MODEL), lambda i, j: (i, 0)),
            pl.BlockSpec((1, D_MODEL), lambda i, j: (0, 0)),
            pl.BlockSpec((D_MODEL, tn), lambda i, j: (0, j)),
            pl.BlockSpec((D_MODEL, LANES), lambda i, j: (0, 0)),
        ],
        out_specs=[
            pl.BlockSpec((tm, tn), lambda i, j: (i, j)),
            pl.BlockSpec((tm, LANES), lambda i, j: (i, 0)),
        ],
        out_shape=[
            jax.ShapeDtypeStruct((t, D_PROJ_MAIN), F32),
            jax.ShapeDtypeStruct((t, LANES), F32),
        ],
        scratch_shapes=[pltpu.VMEM((tm, D_MODEL), BF16)],
        compiler_params=_params(("arbitrary", "arbitrary")),
        name="inproj",
    )(x, g, w_main, w_lr)


def _gelu(x):
    return x * (lax.erf(x * (2.0 ** -0.5)) + 1.0) * 0.5


def _gmlp_body(u_ref, v_ref, vg_ref, ws_ref, bst_ref, y_ref, *, n_chunks):
    row = lax.broadcasted_iota(I32, (GMLP_CHUNK, GMLP_CHUNK), 0)
    col = lax.broadcasted_iota(I32, (GMLP_CHUNK, GMLP_CHUNK), 1)
    causal = row >= col
    for c in range(n_chunks):
        rows = slice(c * GMLP_CHUNK, (c + 1) * GMLP_CHUNK)
        gv = _gelu(v_ref[rows, :])
        mu = jnp.mean(gv, axis=-1, keepdims=True)
        vc = gv - mu
        var = jnp.mean(vc * vc, axis=-1, keepdims=True)
        vn = ((vc * lax.rsqrt(var + NORM_EPS)) * vg_ref[...]).astype(BF16)
        for h in range(GMLP_HEADS):
            cols = slice(h * GMLP_HEAD_DIM, (h + 1) * GMLP_HEAD_DIM)
            wm = jnp.where(causal, ws_ref[h], 0.0).astype(BF16)
            s = jnp.dot(wm, vn[:, cols], preferred_element_type=F32) + bst_ref[:, h:h + 1]
            y_ref[rows, cols] = (_gelu(u_ref[rows, cols]) * s).astype(BF16)


def _gmlp(z, v_g, ws, bs_t, *, tb):
    t = z.shape[0]
    blk = GMLP_WIDTH
    return pl.pallas_call(
        functools.partial(_gmlp_body, n_chunks=tb // GMLP_CHUNK),
        grid=(t // tb,),
        in_specs=[
            pl.BlockSpec((tb, blk), lambda i: (i, 0)),
            pl.BlockSpec((tb, blk), lambda i: (i, 1)),
            pl.BlockSpec((1, blk), lambda i: (0, 0)),
            pl.BlockSpec((GMLP_HEADS, GMLP_CHUNK, GMLP_CHUNK), lambda i: (0, 0, 0)),
            pl.BlockSpec((GMLP_CHUNK, GMLP_HEADS), lambda i: (0, 0)),
        ],
        out_specs=pl.BlockSpec((tb, blk), lambda i: (i, 0)),
        out_shape=jax.ShapeDtypeStruct((t, blk), BF16),
        compiler_params=_params(("arbitrary",)),
        name="gmlp",
    )(z, z, v_g, ws, bs_t)


def _gla_body(q_ref, k_ref, v_ref, r_ref, lr_ref, wa2_ref, ba_ref, gn_ref, y_ref, st_ref, bcum_ref, *, n_chunks):
    @pl.when(pl.program_id(0) == 0)
    def _():
        st_ref[...] = jnp.zeros_like(st_ref)

    c_len = GLA_CHUNK
    tb = lr_ref.shape[0]
    tn_dims = (((0,), (0,)), ((), ()))
    nt_dims = (((1,), (1,)), ((), ()))

    logit = jnp.dot(lr_ref[...].astype(BF16), wa2_ref[...], preferred_element_type=F32) + ba_ref[...]
    log_a = (jnp.minimum(logit, 0.0) - jnp.log1p(jnp.exp(-jnp.abs(logit)))) * (1.0 / GLA_GATE_NORMALIZER)
    row_b = lax.broadcasted_iota(I32, (tb, tb), 0)
    col_b = lax.broadcasted_iota(I32, (tb, tb), 1)
    shift = c_len.bit_length() - 1
    same_chunk = lax.shift_right_logical(row_b, shift) == lax.shift_right_logical(col_b, shift)
    lmat = jnp.where(same_chunk & (row_b >= col_b), 1.0, 0.0).astype(F32)
    bcum_ref[...] = jnp.dot(lmat, log_a, precision=HIGHEST, preferred_element_type=F32)

    row = lax.broadcasted_iota(I32, (c_len, c_len), 0)
    col = lax.broadcasted_iota(I32, (c_len, c_len), 1)
    causal = row >= col

    def chunk(c, carry):
        t0 = pl.multiple_of(c * c_len, c_len)
        rows = pl.ds(t0, c_len)
        for h in range(GLA_HEADS):
            kc = slice(h * GLA_HEAD_DK, (h + 1) * GLA_HEAD_DK)
            vc = slice(h * GLA_HEAD_DV, (h + 1) * GLA_HEAD_DV)
            bcum = bcum_ref[rows, kc]
            b_last = bcum[c_len - 1:c_len, :]
            q = q_ref[rows, kc] * (GLA_HEAD_DK ** -0.5)
            k = k_ref[rows, kc]
            q_e = (q * jnp.exp(bcum)).astype(BF16)
            k_e = (k * jnp.exp(-bcum)).astype(BF16)
            k_end = (k * jnp.exp(b_last - bcum)).astype(BF16)
            v = v_ref[rows, vc].astype(BF16)
            scores = lax.dot_general(q_e, k_e, nt_dims, preferred_element_type=F32)
            scores = jnp.where(causal, scores, 0.0).astype(BF16)
            state_t = st_ref[h]
            o = jnp.dot(scores, v, preferred_element_type=F32)
            o = o + lax.dot_general(q_e, state_t.astype(BF16), nt_dims, preferred_element_type=F32)
            kv_t = lax.dot_general(v, k_end, tn_dims, preferred_element_type=F32)
            st_ref[h] = state_t * jnp.exp(b_last) + kv_t
            o = (o * lax.rsqrt(jnp.mean(o * o, axis=-1, keepdims=True) + NORM_EPS)) * gn_ref[...]
            r = r_ref[rows, vc]
            y_ref[rows, vc] = (o * (r * jax.nn.sigmoid(r))).astype(BF16)
        return carry

    lax.fori_loop(0, n_chunks, chunk, 0)


def _gla(z, lr, wa2p, ba, gn, *, tb):
    t = z.shape[0]
    kw, vw = GLA_KEY_WIDTH, GLA_WIDTH
    return pl.pallas_call(
        functools.partial(_gla_body, n_chunks=tb // GLA_CHUNK),
        grid=(t // tb,),
        in_specs=[
            pl.BlockSpec((tb, kw), lambda i: (i, 4)),
            pl.BlockSpec((tb, kw), lambda i: (i, 5)),
            pl.BlockSpec((tb, vw), lambda i: (i, 3)),
            pl.BlockSpec((tb, vw), lambda i: (i, 4)),
            pl.BlockSpec((tb, LANES), lambda i: (i, 0)),
            pl.BlockSpec((LANES, kw), lambda i: (0, 0)),
            pl.BlockSpec((1, kw), lambda i: (0, 0)),
            pl.BlockSpec((1, GLA_HEAD_DV), lambda i: (0, 0)),
        ],
        out_specs=pl.BlockSpec((tb, vw), lambda i: (i, 0)),
        out_shape=jax.ShapeDtypeStruct((t, vw), BF16),
        scratch_shapes=[pltpu.VMEM((GLA_HEADS, GLA_HEAD_DV, GLA_HEAD_DK), F32),
                        pltpu.VMEM((tb, kw), F32)],
        compiler_params=_params(("arbitrary",)),
        name="gla",
    )(z, z, z, z, lr, wa2p, ba, gn)


def _bf16_bits(x):
    return lax.bitcast_convert_type(x.astype(BF16).astype(F32), U32)


def _outproj_body(ya_ref, yb_ref, x_ref, woa_ref, wob_ref, g2_ref, wr_ref, br_ref,
                  h1_ref, hp_ref, gate_ref, eid_ref, hacc_ref, *, nj, tn):
    j = pl.program_id(1)
    tm = x_ref.shape[0]
    acc = jnp.dot(ya_ref[...], woa_ref[...], preferred_element_type=F32)
    acc = acc + jnp.dot(yb_ref[...], wob_ref[...], preferred_element_type=F32)
    h = x_ref[...] + acc
    h1_ref[...] = h
    hacc_ref[j] = h

    @pl.when(j == nj - 1)
    def _():
        ssq = jnp.sum(hacc_ref[0] * hacc_ref[0], axis=-1, keepdims=True)
        for jj in range(1, nj):
            ssq = ssq + jnp.sum(hacc_ref[jj] * hacc_ref[jj], axis=-1, keepdims=True)
        rstd = lax.rsqrt(ssq * (1.0 / D_MODEL) + NORM_EPS)
        logits = br_ref[...]

        def normed(jj):
            cols = slice(jj * tn, (jj + 1) * tn)
            hn = (hacc_ref[jj] * rstd) * g2_ref[:, cols]
            return hn, jnp.dot(hn.astype(BF16), wr_ref[cols, :], preferred_element_type=F32)

        kb = FFN_KC // tn
        for wb in range(nj // 2):
            q, hh = divmod(wb, kb)
            lo, d_lo = normed(2 * q * kb + hh)
            hi, d_hi = normed((2 * q + 1) * kb + hh)
            logits = logits + d_lo + d_hi
            packed = (_bf16_bits(hi) & jnp.uint32(0xFFFF0000)) | (_bf16_bits(lo) >> 16)
            for c in range(tn // LANES):
                ct = wb * (tn // LANES) + c
                hp_ref[pl.ds(ct, tm, stride=ROW_TILES), :] = packed[:, c * LANES:(c + 1) * LANES]

        lane = lax.broadcasted_iota(I32, logits.shape, 1)
        neg = jnp.float32(-jnp.inf)
        big = jnp.int32(LANES)
        cmask = lane < N_GROUPS
        cl = jnp.where(cmask, logits, neg)
        cmax = jnp.max(cl, axis=-1, keepdims=True)
        g_sel = jnp.min(jnp.where(cl == cmax, lane, big), axis=-1, keepdims=True)
        p_sel = 1.0 / jnp.sum(jnp.where(cmask, jnp.exp(logits - cmax), 0.0), axis=-1, keepdims=True)
        f0 = N_GROUPS + g_sel * EXPERTS_PER_GROUP
        fl = jnp.where((lane >= f0) & (lane < f0 + EXPERTS_PER_GROUP), logits, neg)
        v1 = jnp.max(fl, axis=-1, keepdims=True)
        i1 = jnp.min(jnp.where(fl == v1, lane, big), axis=-1, keepdims=True)
        fl2 = jnp.where(lane == i1, neg, fl)
        v2 = jnp.max(fl2, axis=-1, keepdims=True)
        i2 = jnp.min(jnp.where(fl2 == v2, lane, big), axis=-1, keepdims=True)
        e2 = jnp.exp(v2 - v1)
        w1 = p_sel / (1.0 + e2)
        w2 = p_sel * e2 / (1.0 + e2)
        gate_ref[...] = jnp.where(lane == 0, w1, jnp.where(lane == 1, w2, 0.0))
        eid_ref[...] = jnp.where(lane == 0, i1 - N_GROUPS, jnp.where(lane == 1, i2 - N_GROUPS, 0))


def _outproj(ya, yb, x, wo, g2, wr, br, *, tm, tn):
    t = x.shape[0]
    nj = D_MODEL // tn
    assert FFN_KC % tn == 0 and GMLP_WIDTH == GLA_WIDTH
    return pl.pallas_call(
        functools.partial(_outproj_body, nj=nj, tn=tn),
        grid=(t // tm, nj),
        in_specs=[
            pl.BlockSpec((tm, GMLP_WIDTH), lambda i, j: (i, 0)),
            pl.BlockSpec((tm, GLA_WIDTH), lambda i, j: (i, 0)),
            pl.BlockSpec((tm, tn), lambda i, j: (i, j)),
            pl.BlockSpec((GMLP_WIDTH, tn), lambda i, j: (0, j)),
            pl.BlockSpec((GLA_WIDTH, tn), lambda i, j: (1, j)),
            pl.BlockSpec((1, D_MODEL), lambda i, j: (0, 0)),
            pl.BlockSpec((D_MODEL, LANES), lambda i, j: (0, 0)),
            pl.BlockSpec((1, LANES), lambda i, j: (0, 0)),
        ],
        out_specs=[
            pl.BlockSpec((tm, tn), lambda i, j: (i, j)),
            pl.BlockSpec((tm * ROW_TILES, LANES), lambda i, j: (i, 0)),
            pl.BlockSpec((tm, LANES), lambda i, j: (i, 0)),
            pl.BlockSpec((tm, LANES), lambda i, j: (i, 0)),
        ],
        out_shape=[
            jax.ShapeDtypeStruct((t, D_MODEL), F32),
            jax.ShapeDtypeStruct((t * ROW_TILES, LANES), U32),
            jax.ShapeDtypeStruct((t, LANES), F32),
            jax.ShapeDtypeStruct((t, LANES), I32),
        ],
        scratch_shapes=[pltpu.VMEM((nj, tm, tn), F32)],
        compiler_params=_params(("arbitrary", "arbitrary")),
        name="outproj_router",
    )(ya, yb, x, wo, wo, g2, wr, br)


FFN_UP_CHUNKS = D_MODEL // FFN_KC
FFN_NC = 1024
FFN_DOWN_CHUNKS = D_MODEL // FFN_NC
FFN_CHUNKS = 2 * FFN_UP_CHUNKS + FFN_DOWN_CHUNKS
FFN_NBUF = 3
ROW_UNIT = 128
DMA_UNROLL = 8
assert FFN_CHUNKS % FFN_NBUF == 0 and FFN_KC == D_EXPERT == FFN_NC


def _pack_bf16_pair(lo, hi):
    return (_bf16_bits(hi) & jnp.uint32(0xFFFF0000)) | (_bf16_bits(lo) >> 16)


def _unpack_lo(w):
    return lax.bitcast_convert_type(w << 16, F32)


def _unpack_hi(w):
    return lax.bitcast_convert_type(w & jnp.uint32(0xFFFF0000), F32)


def _ffn_body(sbe_ref, sbr_ref, nsb_ref, tok0_ref, tokn_ref, hp_hbm, wg_hbm, wu_hbm, wd_hbm, ys_ref,
              g_ref, gsem, wbuf_ref, wsem, wb_ref, xk_ref, gacc_ref, uacc_ref, a_ref, *, rows):
    s = pl.program_id(0)
    nsb = nsb_ref[0]
    valid = s < nsb
    slot = lax.rem(s, 2)
    s_next = jnp.minimum(s + 1, sbr_ref.shape[0] - 1)
    word_tiles = FFN_KC // LANES

    def units(n_rows):
        return lax.shift_right_logical(n_rows + (ROW_UNIT - 1), ROW_UNIT.bit_length() - 1)

    n_units = units(sbr_ref[s])

    def row_copy(tok_ref, slot_, i):
        src = hp_hbm.at[pl.ds(pl.multiple_of(tok_ref[0, 0, i] * ROW_TILES, ROW_TILES), ROW_TILES)]
        dst = g_ref.at[slot_, pl.ds(pl.multiple_of(i * ROW_TILES, ROW_TILES), ROW_TILES)]
        return pltpu.make_async_copy(src, dst, gsem.at[slot_])

    def for_each_row(n_units_, fn):
        def unit_body(u, c):
            def body(i, c2):
                fn(u * ROW_UNIT + i)
                return c2
            return lax.fori_loop(0, ROW_UNIT, body, c, unroll=DMA_UNROLL)
        lax.fori_loop(0, n_units_, unit_body, 0)

    def for_row_blocks(fn, after_first):
        u = ROW_UNIT
        big, mid = 4 * u, 2 * u
        for lo, hi, m in ((4, rows // u, big), (2, 3, mid), (1, 1, u)):
            @pl.when((n_units >= lo) & (n_units <= hi))
            def _(m=m):
                fn(0, m)
                after_first()
        for n, r0, m in ((6, big, mid), (5, big, u), (3, mid, u)):
            @pl.when(n_units == n)
            def _(r0=r0, m=m):
                fn(r0, m)

    def chunk_copy(e, ci):
        if ci < 2 * FFN_UP_CHUNKS:
            w_hbm = wg_hbm if ci % 2 == 0 else wu_hbm
            src = w_hbm.at[e, pl.ds((ci // 2) * FFN_KC, FFN_KC), :]
        else:
            src = wd_hbm.at[e, :, pl.ds((ci - 2 * FFN_UP_CHUNKS) * FFN_NC, FFN_NC)]
        b = ci % FFN_NBUF
        return pltpu.make_async_copy(src, wbuf_ref.at[b], wsem.at[b])

    @pl.when(s == 0)
    def _():
        for ci in range(FFN_NBUF):
            chunk_copy(sbe_ref[0], ci).start()
        for_each_row(units(sbr_ref[0]), lambda i: row_copy(tok0_ref, 0, i).start())

    @pl.when(valid)
    def _():
        e_cur = sbe_ref[s]
        has_next = s + 1 < nsb
        for_each_row(n_units, lambda i: row_copy(tok0_ref, slot, i).wait())

        @pl.when(has_next)
        def _():
            for_each_row(units(sbr_ref[s_next]), lambda i: row_copy(tokn_ref, 1 - slot, i).start())

        def to_bf16(ci):
            wb_ref[ci % 2] = wbuf_ref[ci % FFN_NBUF].astype(BF16)

        def refill(ci):
            ahead = ci + FFN_NBUF
            if ahead < FFN_CHUNKS:
                chunk_copy(e_cur, ahead).start()
            else:
                @pl.when(has_next)
                def _():
                    chunk_copy(sbe_ref[s_next], ahead - FFN_CHUNKS).start()

        chunk_copy(e_cur, 0).wait()
        to_bf16(0)
        refill(0)
        for ci in range(FFN_CHUNKS):
            wb = wb_ref.at[ci % 2]
            if ci + 1 < FFN_CHUNKS:
                chunk_copy(e_cur, ci + 1).wait()
                convert_next = functools.partial(to_bf16, ci + 1)
            else:
                convert_next = lambda: None

            if ci < 2 * FFN_UP_CHUNKS:
                k, is_up = divmod(ci, 2)

                def block(r0, m, k=k, is_up=is_up, wb=wb):
                    r = slice(r0, r0 + m)
                    if not is_up and k % 2 == 0:
                        first = r0 * ROW_TILES + (k // 2) * word_tiles
                        w = jnp.concatenate(
                            [g_ref[slot, pl.ds(first + c, m, stride=ROW_TILES), :] for c in range(word_tiles)],
                            axis=1)
                        xk_ref[0, r, :] = _unpack_lo(w).astype(BF16)
                        xk_ref[1, r, :] = _unpack_hi(w).astype(BF16)
                    acc_ref = uacc_ref if is_up else gacc_ref
                    d = jnp.dot(xk_ref[k % 2, r, :], wb[...], preferred_element_type=F32)
                    if k > 0:
                        d = acc_ref[r, :] + d
                    if is_up and k == FFN_UP_CHUNKS - 1:
                        g = gacc_ref[r, :]
                        a_ref[r, :] = ((g * jax.nn.sigmoid(g)) * d).astype(BF16)
                    else:
                        acc_ref[r, :] = d
            else:
                n = ci - 2 * FFN_UP_CHUNKS

                def block(r0, m, n=n, wb=wb):
                    y = jnp.dot(a_ref[r0:r0 + m, :], wb[...], preferred_element_type=F32)
                    half = FFN_NC // 2
                    ys_ref[r0:r0 + m, n * half:(n + 1) * half] = _pack_bf16_pair(y[:, :half], y[:, half:])

            for_row_blocks(block, convert_next)
            if ci + 1 < FFN_CHUNKS:
                refill(ci + 1)

        for u in range(rows // ROW_UNIT):
            @pl.when(n_units <= u)
            def _(u=u):
                ys_ref[u * ROW_UNIT:(u + 1) * ROW_UNIT, :] = jnp.zeros((ROW_UNIT, ys_ref.shape[1]), U32)

    @pl.when(jnp.logical_not(valid))
    def _():
        ys_ref[...] = jnp.zeros_like(ys_ref)


def _ffn(sb_e, sb_rows, nsb, tok3, hp, w_gate, w_up, w_down, *, rows):
    s_max = sb_e.shape[0]
    assert rows == 6 * ROW_UNIT, "the matmul block plan in _ffn_body covers one to six row units"
    return pl.pallas_call(
        functools.partial(_ffn_body, rows=rows),
        grid_spec=pltpu.PrefetchScalarGridSpec(
            num_scalar_prefetch=3,
            grid=(s_max,),
            in_specs=[
                pl.BlockSpec((1, 1, rows), lambda s, sbe, sbr, nsb: (0, 0, 0), memory_space=pltpu.SMEM),
                pl.BlockSpec((1, 1, rows), lambda s, sbe, sbr, nsb: (jnp.minimum(s + 1, s_max - 1), 0, 0),
                             memory_space=pltpu.SMEM),
                pl.BlockSpec(memory_space=pl.ANY),
                pl.BlockSpec(memory_space=pl.ANY),
                pl.BlockSpec(memory_space=pl.ANY),
                pl.BlockSpec(memory_space=pl.ANY),
            ],
            out_specs=pl.BlockSpec((rows, HALF), lambda s, sbe, sbr, nsb: (s, 0)),
            scratch_shapes=[
                pltpu.VMEM((2, rows * ROW_TILES, LANES), U32),
                pltpu.SemaphoreType.DMA((2,)),
                pltpu.VMEM((FFN_NBUF, FFN_KC, D_EXPERT), F32),
                pltpu.SemaphoreType.DMA((FFN_NBUF,)),
                pltpu.VMEM((2, FFN_KC, D_EXPERT), BF16),
                pltpu.VMEM((2, rows, FFN_KC), BF16),
                pltpu.VMEM((rows, D_EXPERT), F32),
                pltpu.VMEM((rows, D_EXPERT), F32),
                pltpu.VMEM((rows, D_EXPERT), BF16),
            ],
        ),
        out_shape=jax.ShapeDtypeStruct((s_max * rows, HALF), U32),
        compiler_params=_params(("arbitrary",)),
        name="moe_ffn",
    )(sb_e, sb_rows, nsb, tok3, tok3, hp, w_gate, w_up, w_down)


def _combine_body(pos0_ref, posn_ref, h1_ref, gate_ref, gf_ref, ys_hbm, out_ref, buf_ref, sem, *, tb):
    i = pl.program_id(0)
    slot = lax.rem(i, 2)

    def row_copy(pos_ref, slot_, k, j):
        src = ys_hbm.at[pl.ds(pos_ref[0, k, j], 1)]
        return pltpu.make_async_copy(src, buf_ref.at[slot_, k, pl.ds(j, 1)], sem.at[slot_])

    def for_each_row(fn):
        def body(j, c):
            fn(0, j)
            fn(1, j)
            return c
        lax.fori_loop(0, tb, body, 0, unroll=DMA_UNROLL)

    @pl.when(i == 0)
    def _():
        for_each_row(lambda k, j: row_copy(pos0_ref, 0, k, j).start())

    for_each_row(lambda k, j: row_copy(pos0_ref, slot, k, j).wait())

    @pl.when(i + 1 < pl.num_programs(0))
    def _():
        for_each_row(lambda k, j: row_copy(posn_ref, 1 - slot, k, j).start())

    gate = gate_ref[...]
    half = FFN_NC // 2
    for n in range(FFN_DOWN_CHUNKS):
        w0 = buf_ref[slot, 0, :, n * half:(n + 1) * half]
        w1 = buf_ref[slot, 1, :, n * half:(n + 1) * half]
        for unpack, c0 in ((_unpack_lo, n * FFN_NC), (_unpack_hi, n * FFN_NC + half)):
            cols = slice(c0, c0 + half)
            y = unpack(w0) * gate[:, 0:1] + unpack(w1) * gate[:, 1:2]
            out_ref[:, cols] = h1_ref[:, cols] + y
    h = out_ref[...]
    ms = jnp.mean(h * h, axis=-1, keepdims=True)
    out_ref[...] = (h * lax.rsqrt(ms + NORM_EPS)) * gf_ref[...]


def _combine(pos3, h1, gate, gf, ys, *, tb):
    t = h1.shape[0]
    n_blocks = t // tb
    return pl.pallas_call(
        functools.partial(_combine_body, tb=tb),
        grid=(n_blocks,),
        in_specs=[
            pl.BlockSpec((1, 2, tb), lambda i: (0, 0, 0), memory_space=pltpu.SMEM),
            pl.BlockSpec((1, 2, tb), lambda i: (jnp.minimum(i + 1, n_blocks - 1), 0, 0), memory_space=pltpu.SMEM),
            pl.BlockSpec((tb, D_MODEL), lambda i: (i, 0)),
            pl.BlockSpec((tb, LANES), lambda i: (i, 0)),
            pl.BlockSpec((1, D_MODEL), lambda i: (0, 0)),
            pl.BlockSpec(memory_space=pl.ANY),
        ],
        out_specs=pl.BlockSpec((tb, D_MODEL), lambda i: (i, 0)),
        out_shape=jax.ShapeDtypeStruct((t, D_MODEL), F32),
        scratch_shapes=[pltpu.VMEM((2, 2, tb, HALF), U32), pltpu.SemaphoreType.DMA((2,))],
        compiler_params=_params(("arbitrary",)),
        name="moe_combine",
    )(pos3, pos3, h1, gate, gf, ys)


def _dispatch_plan(eid, *, rows, s_max):
    t = eid.shape[0]
    flat_e = eid[:, :2].reshape(-1)
    onehot = (flat_e[:, None] == jnp.arange(N_EXPERTS, dtype=I32)[None, :]).astype(I32)
    csum = jnp.cumsum(onehot, axis=0)
    rank = jnp.sum(csum * onehot, axis=1) - 1
    counts = csum[-1]
    nsb_e = (counts + rows - 1) // rows
    sb_end = jnp.cumsum(nsb_e)
    sb_start = sb_end - nsb_e
    nsb = sb_end[-1]
    pos = sb_start[flat_e] * rows + rank
    s_idx = jnp.minimum(jnp.arange(s_max, dtype=I32), nsb - 1)
    sb_e = jnp.minimum(jnp.searchsorted(sb_end, s_idx, side="right"), N_EXPERTS - 1).astype(I32)
    sb_rows = jnp.clip(counts[sb_e] - (s_idx - sb_start[sb_e]) * rows, 0, rows)
    sb_rows = jnp.where(jnp.arange(s_max) < nsb, sb_rows, 0).astype(I32)
    flat_tok = jnp.arange(2 * t, dtype=I32) // 2
    tok = jnp.zeros((s_max * rows,), I32).at[pos].set(flat_tok)
    return sb_e, sb_rows, nsb.reshape(1).astype(I32), tok.reshape(s_max, 1, rows), pos.reshape(t, 2)


def _layer(x, norm1_g, w_in, gmlp_v_g, gmlp_ws, gmlp_bs, gla_wa2, gla_ba, gla_norm_g, w_out, norm2_g,
           router_coarse_w, router_coarse_b, router_fine_w, router_fine_b, exp_w_gate, exp_w_up, exp_w_down,
           norm_f_g, *, tm_in, tb_gmlp, tb_gla, tm_out, tb_comb, sb_rows):
    t = x.shape[0]
    w_main = w_in.astype(BF16)
    w_lr = jnp.pad(w_in[:, D_PROJ_MAIN:], ((0, 0), (0, LANES - GLA_GATE_RANK))).astype(BF16)
    wa2p = jnp.pad(gla_wa2, ((0, LANES - GLA_GATE_RANK), (0, 0))).astype(BF16)
    wo = w_out.astype(BF16)
    n_route = N_GROUPS + N_EXPERTS
    wr = jnp.pad(jnp.concatenate([router_coarse_w, router_fine_w], axis=1),
                 ((0, 0), (0, LANES - n_route))).astype(BF16)
    br = jnp.pad(jnp.concatenate([router_coarse_b, router_fine_b]), (0, LANES - n_route)).reshape(1, LANES)

    z, lr = _inproj(x, norm1_g.reshape(1, -1), w_main, w_lr, tm=tm_in, tn=1024)
    y_a = _gmlp(z, gmlp_v_g.reshape(1, -1), gmlp_ws, gmlp_bs.T, tb=tb_gmlp)
    y_b = _gla(z, lr, wa2p, gla_ba.reshape(1, -1), gla_norm_g.reshape(1, -1), tb=tb_gla)
    h1, hp, gate, eid = _outproj(y_a, y_b, x, wo, norm2_g.reshape(1, -1), wr, br, tm=tm_out, tn=512)

    s_max = N_EXPERTS + (2 * t) // sb_rows
    sb_e, sb_n, nsb, tok3, pos = _dispatch_plan(eid, rows=sb_rows, s_max=s_max)
    ys = _ffn(sb_e, sb_n, nsb, tok3, hp, exp_w_gate, exp_w_up, exp_w_down, rows=sb_rows)
    pos3 = pos.reshape(t // tb_comb, tb_comb, 2).transpose(0, 2, 1)
    return _combine(pos3, h1, gate, norm_f_g.reshape(1, -1), ys, tb=tb_comb)


def kernel(x, norm1_g, w_in, gmlp_v_g, gmlp_ws, gmlp_bs, gla_wa2, gla_ba, gla_norm_g, w_out, norm2_g,
           router_coarse_w, router_coarse_b, router_fine_w, router_fine_b, exp_w_gate, exp_w_up, exp_w_down,
           norm_f_g):
    b, t, d = x.shape
    assert b == 1 and d == D_MODEL and norm1_g.shape[0] == 1, "one sequence, one layer, the stated widths"

    def first(a):
        return a.reshape(a.shape[1:])

    out = _layer(
        first(x), first(norm1_g), first(w_in), first(gmlp_v_g), first(gmlp_ws), first(gmlp_bs), first(gla_wa2),
        first(gla_ba), first(gla_norm_g), first(w_out), first(norm2_g), first(router_coarse_w),
        first(router_coarse_b), first(router_fine_w), first(router_fine_b), first(exp_w_gate), first(exp_w_up),
        first(exp_w_down), norm_f_g,
        tm_in=512, tb_gmlp=256, tb_gla=512, tm_out=512, tb_comb=512, sb_rows=768)
    return out.reshape(b, t, d)
```

```python
import functools

import jax
import jax.numpy as jnp
from jax import lax
from jax.experimental import pallas as pl
from jax.experimental.pallas import tpu as pltpu

F32 = jnp.float32
BF16 = jnp.bfloat16
U32 = jnp.uint32
I32 = jnp.int32

D_MODEL = 4096
GMLP_WIDTH = 2048
GMLP_HEADS = 8
GMLP_HEAD_DIM = GMLP_WIDTH // GMLP_HEADS
GMLP_CHUNK = 128
GLA_WIDTH = 2048
GLA_HEADS = 4
GLA_KEY_WIDTH = 1024
GLA_HEAD_DK = GLA_KEY_WIDTH // GLA_HEADS
GLA_HEAD_DV = GLA_WIDTH // GLA_HEADS
GLA_GATE_RANK = 16
GLA_GATE_NORMALIZER = 16.0
GLA_CHUNK = 64
N_GROUPS = 8
EXPERTS_PER_GROUP = 8
N_EXPERTS = N_GROUPS * EXPERTS_PER_GROUP
D_EXPERT = D_MODEL // 4
NORM_EPS = 1e-6
D_PROJ_MAIN = 2 * GMLP_WIDTH + 2 * GLA_KEY_WIDTH + 2 * GLA_WIDTH

LANES = 128
HALF = D_MODEL // 2
ROW_TILES = HALF // LANES
FFN_KC = 1024
VMEM_LIMIT = 58 * 1024 * 1024

HIGHEST = lax.Precision.HIGHEST


def _params(semantics, vmem=VMEM_LIMIT):
    return pltpu.CompilerParams(dimension_semantics=semantics, vmem_limit_bytes=vmem)


def _inproj_body(x_ref, g_ref, w_ref, wlr_ref, z_ref, lr_ref, xn_ref):
    @pl.when(pl.program_id(1) == 0)
    def _():
        x = x_ref[...]
        ms = jnp.mean(x * x, axis=-1, keepdims=True)
        xn = ((x * lax.rsqrt(ms + NORM_EPS)) * g_ref[...]).astype(BF16)
        xn_ref[...] = xn
        lr_ref[...] = jnp.dot(xn, wlr_ref[...], preferred_element_type=F32)

    z_ref[...] = jnp.dot(xn_ref[...], w_ref[...], preferred_element_type=F32)


def _inproj(x, g, w_main, w_lr, *, tm, tn):
    t = x.shape[0]
    return pl.pallas_call(
        _inproj_body,
        grid=(t // tm, D_PROJ_MAIN // tn),
        in_specs=[
            pl.BlockSpec((tm, D_MODEL), lambda i, j: (i, 0)),
            pl.BlockSpec((1, D_MODEL), lambda i, j: (0, 0)),
            pl.BlockSpec((D_MODEL, tn), lambda i, j: (0, j)),
            pl.BlockSpec((D_MODEL, LANES), lambda i, j: (0, 0)),
        ],
        out_specs=[
            pl.BlockSpec((tm, tn), lambda i, j: (i, j)),
            pl.BlockSpec((tm, LANES), lambda i, j: (i, 0)),
        ],
        out_shape=[
            jax.ShapeDtypeStruct((t, D_PROJ_MAIN), F32),
            jax.ShapeDtypeStruct((t, LANES), F32),
        ],
        scratch_shapes=[pltpu.VMEM((tm, D_MODEL), BF16)],
        compiler_params=_params(("arbitrary", "arbitrary")),
        name="inproj",
    )(x, g, w_main, w_lr)


def _gelu(x):
    return x * (lax.erf(x * (2.0 ** -0.5)) + 1.0) * 0.5


def _gmlp_body(u_ref, v_ref, vg_ref, ws_ref, bst_ref, y_ref, *, n_chunks):
    row = lax.broadcasted_iota(I32, (GMLP_CHUNK, GMLP_CHUNK), 0)
    col = lax.broadcasted_iota(I32, (GMLP_CHUNK, GMLP_CHUNK), 1)
    causal = row >= col
    for c in range(n_chunks):
        rows = slice(c * GMLP_CHUNK, (c + 1) * GMLP_CHUNK)
        gv = _gelu(v_ref[rows, :])
        mu = jnp.mean(gv, axis=-1, keepdims=True)
        vc = gv - mu
        var = jnp.mean(vc * vc, axis=-1, keepdims=True)
        vn = ((vc * lax.rsqrt(var + NORM_EPS)) * vg_ref[...]).astype(BF16)
        for h in range(GMLP_HEADS):
            cols = slice(h * GMLP_HEAD_DIM, (h + 1) * GMLP_HEAD_DIM)
            wm = jnp.where(causal, ws_ref[h], 0.0).astype(BF16)
            s = jnp.dot(wm, vn[:, cols], preferred_element_type=F32) + bst_ref[:, h:h + 1]
            y_ref[rows, cols] = (_gelu(u_ref[rows, cols]) * s).astype(BF16)


def _gmlp(z, v_g, ws, bs_t, *, tb):
    t = z.shape[0]
    blk = GMLP_WIDTH
    return pl.pallas_call(
        functools.partial(_gmlp_body, n_chunks=tb // GMLP_CHUNK),
        grid=(t // tb,),
        in_specs=[
            pl.BlockSpec((tb, blk), lambda i: (i, 0)),
            pl.BlockSpec((tb, blk), lambda i: (i, 1)),
            pl.BlockSpec((1, blk), lambda i: (0, 0)),
            pl.BlockSpec((GMLP_HEADS, GMLP_CHUNK, GMLP_CHUNK), lambda i: (0, 0, 0)),
            pl.BlockSpec((GMLP_CHUNK, GMLP_HEADS), lambda i: (0, 0)),
        ],
        out_specs=pl.BlockSpec((tb, blk), lambda i: (i, 0)),
        out_shape=jax.ShapeDtypeStruct((t, blk), BF16),
        compiler_params=_params(("arbitrary",)),
        name="gmlp",
    )(z, z, v_g, ws, bs_t)


def _gla_body(q_ref, k_ref, v_ref, r_ref, lr_ref, wa2_ref, ba_ref, gn_ref, y_ref, st_ref, bcum_ref, *, n_chunks):
    @pl.when(pl.program_id(0) == 0)
    def _():
        st_ref[...] = jnp.zeros_like(st_ref)

    c_len = GLA_CHUNK
    tb = lr_ref.shape[0]
    tn_dims = (((0,), (0,)), ((), ()))
    nt_dims = (((1,), (1,)), ((), ()))

    logit = jnp.dot(lr_ref[...].astype(BF16), wa2_ref[...], preferred_element_type=F32) + ba_ref[...]
    bcum = (jnp.minimum(logit, 0.0) - jnp.log1p(jnp.exp(-jnp.abs(logit)))) * (1.0 / GLA_GATE_NORMALIZER)
    row_in_chunk = lax.broadcasted_iota(I32, (tb, 1), 0) & (c_len - 1)
    step = 1
    while step < c_len:
        bcum = bcum + jnp.where(row_in_chunk >= step, pltpu.roll(bcum, step, axis=0), 0.0)
        step *= 2
    bcum_ref[...] = bcum

    row = lax.broadcasted_iota(I32, (c_len, c_len), 0)
    col = lax.broadcasted_iota(I32, (c_len, c_len), 1)
    causal = row >= col

    def chunk(c, carry):
        t0 = pl.multiple_of(c * c_len, c_len)
        rows = pl.ds(t0, c_len)
        for h in range(GLA_HEADS):
            kc = slice(h * GLA_HEAD_DK, (h + 1) * GLA_HEAD_DK)
            vc = slice(h * GLA_HEAD_DV, (h + 1) * GLA_HEAD_DV)
            bcum = bcum_ref[rows, kc]
            b_last = bcum[c_len - 1:c_len, :]
            q = q_ref[rows, kc] * (GLA_HEAD_DK ** -0.5)
            k = k_ref[rows, kc]
            q_e = (q * jnp.exp(bcum)).astype(BF16)
            k_e = (k * jnp.exp(-bcum)).astype(BF16)
            k_end = (k * jnp.exp(b_last - bcum)).astype(BF16)
            v = v_ref[rows, vc].astype(BF16)
            scores = lax.dot_general(q_e, k_e, nt_dims, preferred_element_type=F32)
            scores = jnp.where(causal, scores, 0.0).astype(BF16)
            state_t = st_ref[h]
            o = jnp.dot(scores, v, preferred_element_type=F32)
            o = o + lax.dot_general(q_e, state_t.astype(BF16), nt_dims, preferred_element_type=F32)
            kv_t = lax.dot_general(v, k_end, tn_dims, preferred_element_type=F32)
            st_ref[h] = state_t * jnp.exp(b_last) + kv_t
            o = (o * lax.rsqrt(jnp.mean(o * o, axis=-1, keepdims=True) + NORM_EPS)) * gn_ref[...]
            r = r_ref[rows, vc]
            y_ref[rows, vc] = (o * (r * jax.nn.sigmoid(r))).astype(BF16)
        return carry

    lax.fori_loop(0, n_chunks, chunk, 0)


def _gla(z, lr, wa2p, ba, gn, *, tb):
    t = z.shape[0]
    kw, vw = GLA_KEY_WIDTH, GLA_WIDTH
    return pl.pallas_call(
        functools.partial(_gla_body, n_chunks=tb // GLA_CHUNK),
        grid=(t // tb,),
        in_specs=[
            pl.BlockSpec((tb, kw), lambda i: (i, 4)),
            pl.BlockSpec((tb, kw), lambda i: (i, 5)),
            pl.BlockSpec((tb, vw), lambda i: (i, 3)),
            pl.BlockSpec((tb, vw), lambda i: (i, 4)),
            pl.BlockSpec((tb, LANES), lambda i: (i, 0)),
            pl.BlockSpec((LANES, kw), lambda i: (0, 0)),
            pl.BlockSpec((1, kw), lambda i: (0, 0)),
            pl.BlockSpec((1, GLA_HEAD_DV), lambda i: (0, 0)),
        ],
        out_specs=pl.BlockSpec((tb, vw), lambda i: (i, 0)),
        out_shape=jax.ShapeDtypeStruct((t, vw), BF16),
        scratch_shapes=[pltpu.VMEM((GLA_HEADS, GLA_HEAD_DV, GLA_HEAD_DK), F32),
                        pltpu.VMEM((tb, kw), F32)],
        compiler_params=_params(("arbitrary",)),
        name="gla",
    )(z, z, z, z, lr, wa2p, ba, gn)


def _bf16_bits(x):
    return lax.bitcast_convert_type(x.astype(BF16).astype(F32), U32)


def _outproj_body(ya_ref, yb_ref, x_ref, woa_ref, wob_ref, g2_ref, wr_ref, br_ref,
                  h1_ref, hp_ref, gate_ref, eid_ref, hacc_ref, *, nj, tn):
    j = pl.program_id(1)
    tm = x_ref.shape[0]
    acc = jnp.dot(ya_ref[...], woa_ref[...], preferred_element_type=F32)
    acc = acc + jnp.dot(yb_ref[...], wob_ref[...], preferred_element_type=F32)
    h = x_ref[...] + acc
    h1_ref[...] = h
    hacc_ref[j] = h

    @pl.when(j == nj - 1)
    def _():
        ssq = jnp.sum(hacc_ref[0] * hacc_ref[0], axis=-1, keepdims=True)
        for jj in range(1, nj):
            ssq = ssq + jnp.sum(hacc_ref[jj] * hacc_ref[jj], axis=-1, keepdims=True)
        rstd = lax.rsqrt(ssq * (1.0 / D_MODEL) + NORM_EPS)
        logits = br_ref[...]

        def normed(jj):
            cols = slice(jj * tn, (jj + 1) * tn)
            hn = (hacc_ref[jj] * rstd) * g2_ref[:, cols]
            return hn, jnp.dot(hn.astype(BF16), wr_ref[cols, :], preferred_element_type=F32)

        kb = FFN_KC // tn
        for wb in range(nj // 2):
            q, hh = divmod(wb, kb)
            lo, d_lo = normed(2 * q * kb + hh)
            hi, d_hi = normed((2 * q + 1) * kb + hh)
            logits = logits + d_lo + d_hi
            packed = (_bf16_bits(hi) & jnp.uint32(0xFFFF0000)) | (_bf16_bits(lo) >> 16)
            for c in range(tn // LANES):
                ct = wb * (tn // LANES) + c
                hp_ref[pl.ds(ct, tm, stride=ROW_TILES), :] = packed[:, c * LANES:(c + 1) * LANES]

        lane = lax.broadcasted_iota(I32, logits.shape, 1)
        neg = jnp.float32(-jnp.inf)
        big = jnp.int32(LANES)
        cmask = lane < N_GROUPS
        cl = jnp.where(cmask, logits, neg)
        cmax = jnp.max(cl, axis=-1, keepdims=True)
        g_sel = jnp.min(jnp.where(cl == cmax, lane, big), axis=-1, keepdims=True)
        p_sel = 1.0 / jnp.sum(jnp.where(cmask, jnp.exp(logits - cmax), 0.0), axis=-1, keepdims=True)
        f0 = N_GROUPS + g_sel * EXPERTS_PER_GROUP
        fl = jnp.where((lane >= f0) & (lane < f0 + EXPERTS_PER_GROUP), logits, neg)
        v1 = jnp.max(fl, axis=-1, keepdims=True)
        i1 = jnp.min(jnp.where(fl == v1, lane, big), axis=-1, keepdims=True)
        fl2 = jnp.where(lane == i1, neg, fl)
        v2 = jnp.max(fl2, axis=-1, keepdims=True)
        i2 = jnp.min(jnp.where(fl2 == v2, lane, big), axis=-1, keepdims=True)
        e2 = jnp.exp(v2 - v1)
        w1 = p_sel / (1.0 + e2)
        w2 = p_sel * e2 / (1.0 + e2)
        gate_ref[...] = jnp.where(lane == 0, w1, jnp.where(lane == 1, w2, 0.0))
        eid_ref[...] = jnp.where(lane == 0, i1 - N_GROUPS, jnp.where(lane == 1, i2 - N_GROUPS, 0))


def _outproj(ya, yb, x, wo, g2, wr, br, *, tm, tn):
    t = x.shape[0]
    nj = D_MODEL // tn
    assert FFN_KC % tn == 0 and GMLP_WIDTH == GLA_WIDTH
    return pl.pallas_call(
        functools.partial(_outproj_body, nj=nj, tn=tn),
        grid=(t // tm, nj),
        in_specs=[
            pl.BlockSpec((tm, GMLP_WIDTH), lambda i, j: (i, 0)),
            pl.BlockSpec((tm, GLA_WIDTH), lambda i, j: (i, 0)),
            pl.BlockSpec((tm, tn), lambda i, j: (i, j)),
            pl.BlockSpec((GMLP_WIDTH, tn), lambda i, j: (0, j)),
            pl.BlockSpec((GLA_WIDTH, tn), lambda i, j: (1, j)),
            pl.BlockSpec((1, D_MODEL), lambda i, j: (0, 0)),
            pl.BlockSpec((D_MODEL, LANES), lambda i, j: (0, 0)),
            pl.BlockSpec((1, LANES), lambda i, j: (0, 0)),
        ],
        out_specs=[
            pl.BlockSpec((tm, tn), lambda i, j: (i, j)),
            pl.BlockSpec((tm * ROW_TILES, LANES), lambda i, j: (i, 0)),
            pl.BlockSpec((tm, LANES), lambda i, j: (i, 0)),
            pl.BlockSpec((tm, LANES), lambda i, j: (i, 0)),
        ],
        out_shape=[
            jax.ShapeDtypeStruct((t, D_MODEL), F32),
            jax.ShapeDtypeStruct((t * ROW_TILES, LANES), U32),
            jax.ShapeDtypeStruct((t, LANES), F32),
            jax.ShapeDtypeStruct((t, LANES), I32),
        ],
        scratch_shapes=[pltpu.VMEM((nj, tm, tn), F32)],
        compiler_params=_params(("arbitrary", "arbitrary")),
        name="outproj_router",
    )(ya, yb, x, wo, wo, g2, wr, br)


FFN_UP_CHUNKS = D_MODEL // FFN_KC
FFN_NC = 1024
FFN_DOWN_CHUNKS = D_MODEL // FFN_NC
FFN_CHUNKS = 2 * FFN_UP_CHUNKS + FFN_DOWN_CHUNKS
FFN_NBUF = 3
ROW_UNIT = 128
DMA_UNROLL = 8
assert FFN_CHUNKS % FFN_NBUF == 0 and FFN_KC == D_EXPERT == FFN_NC


def _pack_bf16_pair(lo, hi):
    return (_bf16_bits(hi) & jnp.uint32(0xFFFF0000)) | (_bf16_bits(lo) >> 16)


def _unpack_lo(w):
    return lax.bitcast_convert_type(w << 16, F32)


def _unpack_hi(w):
    return lax.bitcast_convert_type(w & jnp.uint32(0xFFFF0000), F32)


def _ffn_body(sbe_ref, sbr_ref, nsb_ref, tok0_ref, tokn_ref, hp_hbm, wg_hbm, wu_hbm, wd_hbm, ys_ref,
              g_ref, gsem, wbuf_ref, wsem, wb_ref, xk_ref, gacc_ref, uacc_ref, a_ref, *, rows):
    s = pl.program_id(0)
    nsb = nsb_ref[0]
    valid = s < nsb
    slot = lax.rem(s, 2)
    s_next = jnp.minimum(s + 1, sbr_ref.shape[0] - 1)
    word_tiles = FFN_KC // LANES

    def units(n_rows):
        return lax.shift_right_logical(n_rows + (ROW_UNIT - 1), ROW_UNIT.bit_length() - 1)

    n_units = units(sbr_ref[s])

    def row_copy(tok_ref, slot_, i):
        src = hp_hbm.at[pl.ds(pl.multiple_of(tok_ref[0, 0, i] * ROW_TILES, ROW_TILES), ROW_TILES)]
        dst = g_ref.at[slot_, pl.ds(pl.multiple_of(i * ROW_TILES, ROW_TILES), ROW_TILES)]
        return pltpu.make_async_copy(src, dst, gsem.at[slot_])

    def for_each_row(n_units_, fn):
        def unit_body(u, c):
            def body(i, c2):
                fn(u * ROW_UNIT + i)
                return c2
            return lax.fori_loop(0, ROW_UNIT, body, c, unroll=DMA_UNROLL)
        lax.fori_loop(0, n_units_, unit_body, 0)

    def for_row_blocks(fn):
        u = ROW_UNIT
        big, mid = 4 * u, 2 * u
        for lo, hi, m in ((4, rows // u, big), (2, 3, mid), (1, 1, u)):
            @pl.when((n_units >= lo) & (n_units <= hi))
            def _(m=m):
                fn(0, m)
        for n, r0, m in ((6, big, mid), (5, big, u), (3, mid, u)):
            @pl.when(n_units == n)
            def _(r0=r0, m=m):
                fn(r0, m)

    def chunk_copy(e, ci):
        if ci < 2 * FFN_UP_CHUNKS:
            w_hbm = wg_hbm if ci % 2 == 0 else wu_hbm
            src = w_hbm.at[e, pl.ds((ci // 2) * FFN_KC, FFN_KC), :]
        else:
            src = wd_hbm.at[e, :, pl.ds((ci - 2 * FFN_UP_CHUNKS) * FFN_NC, FFN_NC)]
        b = ci % FFN_NBUF
        return pltpu.make_async_copy(src, wbuf_ref.at[b], wsem.at[b])

    @pl.when(s == 0)
    def _():
        for ci in range(FFN_NBUF):
            chunk_copy(sbe_ref[0], ci).start()
        for_each_row(units(sbr_ref[0]), lambda i: row_copy(tok0_ref, 0, i).start())

    @pl.when(valid)
    def _():
        e_cur = sbe_ref[s]
        has_next = s + 1 < nsb
        for_each_row(n_units, lambda i: row_copy(tok0_ref, slot, i).wait())

        @pl.when(has_next)
        def _():
            for_each_row(units(sbr_ref[s_next]), lambda i: row_copy(tokn_ref, 1 - slot, i).start())

        for ci in range(FFN_CHUNKS):
            chunk_copy(e_cur, ci).wait()
            wb = wb_ref.at[ci % 2]
            wb[...] = wbuf_ref[ci % FFN_NBUF].astype(BF16)
            ahead = ci + FFN_NBUF
            if ahead < FFN_CHUNKS:
                chunk_copy(e_cur, ahead).start()
            else:
                @pl.when(has_next)
                def _(ahead=ahead):
                    chunk_copy(sbe_ref[s_next], ahead - FFN_CHUNKS).start()

            if ci < 2 * FFN_UP_CHUNKS:
                k, is_up = divmod(ci, 2)

                def block(r0, m, k=k, is_up=is_up, wb=wb):
                    r = slice(r0, r0 + m)
                    if not is_up and k % 2 == 0:
                        first = r0 * ROW_TILES + (k // 2) * word_tiles
                        w = jnp.concatenate(
                            [g_ref[slot, pl.ds(first + c, m, stride=ROW_TILES), :] for c in range(word_tiles)],
                            axis=1)
                        xk_ref[0, r, :] = _unpack_lo(w).astype(BF16)
                        xk_ref[1, r, :] = _unpack_hi(w).astype(BF16)
                    acc_ref = uacc_ref if is_up else gacc_ref
                    d = jnp.dot(xk_ref[k % 2, r, :], wb[...], preferred_element_type=F32)
                    if k > 0:
                        d = acc_ref[r, :] + d
                    if is_up and k == FFN_UP_CHUNKS - 1:
                        g = gacc_ref[r, :]
                        a_ref[r, :] = ((g * jax.nn.sigmoid(g)) * d).astype(BF16)
                    else:
                        acc_ref[r, :] = d
            else:
                n = ci - 2 * FFN_UP_CHUNKS

                def block(r0, m, n=n, wb=wb):
                    y = jnp.dot(a_ref[r0:r0 + m, :], wb[...], preferred_element_type=F32)
                    half = FFN_NC // 2
                    ys_ref[r0:r0 + m, n * half:(n + 1) * half] = _pack_bf16_pair(y[:, :half], y[:, half:])

            for_row_blocks(block)

        for u in range(rows // ROW_UNIT):
            @pl.when(n_units <= u)
            def _(u=u):
                ys_ref[u * ROW_UNIT:(u + 1) * ROW_UNIT, :] = jnp.zeros((ROW_UNIT, ys_ref.shape[1]), U32)

    @pl.when(jnp.logical_not(valid))
    def _():
        ys_ref[...] = jnp.zeros_like(ys_ref)


def _ffn(sb_e, sb_rows, nsb, tok3, hp, w_gate, w_up, w_down, *, rows):
    s_max = sb_e.shape[0]
    assert rows == 6 * ROW_UNIT, "the matmul block plan in _ffn_body covers one to six row units"
    return pl.pallas_call(
        functools.partial(_ffn_body, rows=rows),
        grid_spec=pltpu.PrefetchScalarGridSpec(
            num_scalar_prefetch=3,
            grid=(s_max,),
            in_specs=[
                pl.BlockSpec((1, 1, rows), lambda s, sbe, sbr, nsb: (0, 0, 0), memory_space=pltpu.SMEM),
                pl.BlockSpec((1, 1, rows), lambda s, sbe, sbr, nsb: (jnp.minimum(s + 1, s_max - 1), 0, 0),
                             memory_space=pltpu.SMEM),
                pl.BlockSpec(memory_space=pl.ANY),
                pl.BlockSpec(memory_space=pl.ANY),
                pl.BlockSpec(memory_space=pl.ANY),
                pl.BlockSpec(memory_space=pl.ANY),
            ],
            out_specs=pl.BlockSpec((rows, HALF), lambda s, sbe, sbr, nsb: (s, 0)),
            scratch_shapes=[
                pltpu.VMEM((2, rows * ROW_TILES, LANES), U32),
                pltpu.SemaphoreType.DMA((2,)),
                pltpu.VMEM((FFN_NBUF, FFN_KC, D_EXPERT), F32),
                pltpu.SemaphoreType.DMA((FFN_NBUF,)),
                pltpu.VMEM((2, FFN_KC, D_EXPERT), BF16),
                pltpu.VMEM((2, rows, FFN_KC), BF16),
                pltpu.VMEM((rows, D_EXPERT), F32),
                pltpu.VMEM((rows, D_EXPERT), F32),
                pltpu.VMEM((rows, D_EXPERT), BF16),
            ],
        ),
        out_shape=jax.ShapeDtypeStruct((s_max * rows, HALF), U32),
        compiler_params=_params(("arbitrary",)),
        name="moe_ffn",
    )(sb_e, sb_rows, nsb, tok3, tok3, hp, w_gate, w_up, w_down)


def _combine_body(pos0_ref, posn_ref, h1_ref, gate_ref, gf_ref, ys_hbm, out_ref, buf_ref, sem, *, tb):
    i = pl.program_id(0)
    slot = lax.rem(i, 2)

    def row_copy(pos_ref, slot_, k, j):
        src = ys_hbm.at[pl.ds(pos_ref[0, k, j], 1)]
        return pltpu.make_async_copy(src, buf_ref.at[slot_, k, pl.ds(j, 1)], sem.at[slot_])

    def for_each_row(fn):
        def body(j, c):
            fn(0, j)
            fn(1, j)
            return c
        lax.fori_loop(0, tb, body, 0, unroll=DMA_UNROLL)

    @pl.when(i == 0)
    def _():
        for_each_row(lambda k, j: row_copy(pos0_ref, 0, k, j).start())

    for_each_row(lambda k, j: row_copy(pos0_ref, slot, k, j).wait())

    @pl.when(i + 1 < pl.num_programs(0))
    def _():
        for_each_row(lambda k, j: row_copy(posn_ref, 1 - slot, k, j).start())

    gate = gate_ref[...]
    half = FFN_NC // 2
    for n in range(FFN_DOWN_CHUNKS):
        w0 = buf_ref[slot, 0, :, n * half:(n + 1) * half]
        w1 = buf_ref[slot, 1, :, n * half:(n + 1) * half]
        for unpack, c0 in ((_unpack_lo, n * FFN_NC), (_unpack_hi, n * FFN_NC + half)):
            cols = slice(c0, c0 + half)
            y = unpack(w0) * gate[:, 0:1] + unpack(w1) * gate[:, 1:2]
            out_ref[:, cols] = h1_ref[:, cols] + y
    h = out_ref[...]
    ms = jnp.mean(h * h, axis=-1, keepdims=True)
    out_ref[...] = (h * lax.rsqrt(ms + NORM_EPS)) * gf_ref[...]


def _combine(pos3, h1, gate, gf, ys, *, tb):
    t = h1.shape[0]
    n_blocks = t // tb
    return pl.pallas_call(
        functools.partial(_combine_body, tb=tb),
        grid=(n_blocks,),
        in_specs=[
            pl.BlockSpec((1, 2, tb), lambda i: (0, 0, 0), memory_space=pltpu.SMEM),
            pl.BlockSpec((1, 2, tb), lambda i: (jnp.minimum(i + 1, n_blocks - 1), 0, 0), memory_space=pltpu.SMEM),
            pl.BlockSpec((tb, D_MODEL), lambda i: (i, 0)),
            pl.BlockSpec((tb, LANES), lambda i: (i, 0)),
            pl.BlockSpec((1, D_MODEL), lambda i: (0, 0)),
            pl.BlockSpec(memory_space=pl.ANY),
        ],
        out_specs=pl.BlockSpec((tb, D_MODEL), lambda i: (i, 0)),
        out_shape=jax.ShapeDtypeStruct((t, D_MODEL), F32),
        scratch_shapes=[pltpu.VMEM((2, 2, tb, HALF), U32), pltpu.SemaphoreType.DMA((2,))],
        compiler_params=_params(("arbitrary",)),
        name="moe_combine",
    )(pos3, pos3, h1, gate, gf, ys)


def _rank_body(eid_ref, rank_ref, counts_ref, carry_ref):
    @pl.when(pl.program_id(0) == 0)
    def _():
        carry_ref[...] = jnp.zeros_like(carry_ref)

    tb = eid_ref.shape[0]
    eid = eid_ref[...]
    lane = lax.broadcasted_iota(I32, eid.shape, 1)
    oh0 = lane == eid[:, 0:1]
    oh1 = lane == eid[:, 1:2]
    hits = jnp.where(oh0 | oh1, 1.0, 0.0)
    row = lax.broadcasted_iota(I32, (tb, tb), 0)
    col = lax.broadcasted_iota(I32, (tb, tb), 1)
    earlier = jnp.where(col < row, 1.0, 0.0).astype(BF16)
    before = jnp.dot(earlier, hits.astype(BF16), preferred_element_type=F32) + carry_ref[...]
    r0 = jnp.sum(jnp.where(oh0, before, 0.0), axis=-1, keepdims=True)
    r1 = jnp.sum(jnp.where(oh1, before, 0.0), axis=-1, keepdims=True)
    rank_ref[...] = jnp.where(lane == 0, r0, jnp.where(lane == 1, r1, 0.0)).astype(I32)
    carry_ref[...] = carry_ref[...] + jnp.sum(hits, axis=0, keepdims=True)
    counts_ref[...] = carry_ref[...].astype(I32)


def _rank(eid, *, tb):
    t = eid.shape[0]
    return pl.pallas_call(
        _rank_body,
        grid=(t // tb,),
        in_specs=[pl.BlockSpec((tb, LANES), lambda i: (i, 0))],
        out_specs=[pl.BlockSpec((tb, LANES), lambda i: (i, 0)), pl.BlockSpec((1, LANES), lambda i: (0, 0))],
        out_shape=[jax.ShapeDtypeStruct((t, LANES), I32), jax.ShapeDtypeStruct((1, LANES), I32)],
        scratch_shapes=[pltpu.VMEM((1, LANES), F32)],
        compiler_params=_params(("arbitrary",)),
        name="moe_rank",
    )(eid)


def _dispatch_plan(eid, *, rows, s_max):
    t = eid.shape[0]
    rank2, counts_row = _rank(eid, tb=512)
    flat_e = eid[:, :2].reshape(-1)
    rank = rank2[:, :2].reshape(-1)
    counts = counts_row[0, :N_EXPERTS]
    nsb_e = (counts + rows - 1) // rows
    sb_end = jnp.cumsum(nsb_e)
    sb_start = sb_end - nsb_e
    nsb = sb_end[-1]
    pos = sb_start[flat_e] * rows + rank
    s_idx = jnp.minimum(jnp.arange(s_max, dtype=I32), nsb - 1)
    sb_e = jnp.minimum(jnp.searchsorted(sb_end, s_idx, side="right"), N_EXPERTS - 1).astype(I32)
    sb_rows = jnp.clip(counts[sb_e] - (s_idx - sb_start[sb_e]) * rows, 0, rows)
    sb_rows = jnp.where(jnp.arange(s_max) < nsb, sb_rows, 0).astype(I32)
    flat_tok = jnp.arange(2 * t, dtype=I32) // 2
    tok = jnp.zeros((s_max * rows,), I32).at[pos].set(flat_tok)
    return sb_e, sb_rows, nsb.reshape(1).astype(I32), tok.reshape(s_max, 1, rows), pos.reshape(t, 2)


def _layer(x, norm1_g, w_in, gmlp_v_g, gmlp_ws, gmlp_bs, gla_wa2, gla_ba, gla_norm_g, w_out, norm2_g,
           router_coarse_w, router_coarse_b, router_fine_w, router_fine_b, exp_w_gate, exp_w_up, exp_w_down,
           norm_f_g, *, tm_in, tb_gmlp, tb_gla, tm_out, tb_comb, sb_rows):
    t = x.shape[0]
    w_main = w_in.astype(BF16)
    w_lr = jnp.pad(w_in[:, D_PROJ_MAIN:], ((0, 0), (0, LANES - GLA_GATE_RANK))).astype(BF16)
    wa2p = jnp.pad(gla_wa2, ((0, LANES - GLA_GATE_RANK), (0, 0))).astype(BF16)
    wo = w_out.astype(BF16)
    n_route = N_GROUPS + N_EXPERTS
    wr = jnp.pad(jnp.concatenate([router_coarse_w, router_fine_w], axis=1),
                 ((0, 0), (0, LANES - n_route))).astype(BF16)
    br = jnp.pad(jnp.concatenate([router_coarse_b, router_fine_b]), (0, LANES - n_route)).reshape(1, LANES)

    z, lr = _inproj(x, norm1_g.reshape(1, -1), w_main, w_lr, tm=tm_in, tn=1024)
    y_a = _gmlp(z, gmlp_v_g.reshape(1, -1), gmlp_ws, gmlp_bs.T, tb=tb_gmlp)
    y_b = _gla(z, lr, wa2p, gla_ba.reshape(1, -1), gla_norm_g.reshape(1, -1), tb=tb_gla)
    h1, hp, gate, eid = _outproj(y_a, y_b, x, wo, norm2_g.reshape(1, -1), wr, br, tm=tm_out, tn=512)

    s_max = N_EXPERTS + (2 * t) // sb_rows
    sb_e, sb_n, nsb, tok3, pos = _dispatch_plan(eid, rows=sb_rows, s_max=s_max)
    ys = _ffn(sb_e, sb_n, nsb, tok3, hp, exp_w_gate, exp_w_up, exp_w_down, rows=sb_rows)
    pos3 = pos.reshape(t // tb_comb, tb_comb, 2).transpose(0, 2, 1)
    return _combine(pos3, h1, gate, norm_f_g.reshape(1, -1), ys, tb=tb_comb)


def kernel(x, norm1_g, w_in, gmlp_v_g, gmlp_ws, gmlp_bs, gla_wa2, gla_ba, gla_norm_g, w_out, norm2_g,
           router_coarse_w, router_coarse_b, router_fine_w, router_fine_b, exp_w_gate, exp_w_up, exp_w_down,
           norm_f_g):
    b, t, d = x.shape
    assert b == 1 and d == D_MODEL and norm1_g.shape[0] == 1, "one sequence, one layer, the stated widths"

    def first(a):
        return a.reshape(a.shape[1:])

    out = _layer(
        first(x), first(norm1_g), first(w_in), first(gmlp_v_g), first(gmlp_ws), first(gmlp_bs), first(gla_wa2),
        first(gla_ba), first(gla_norm_g), first(w_out), first(norm2_g), first(router_coarse_w),
        first(router_coarse_b), first(router_fine_w), first(router_fine_b), first(exp_w_gate), first(exp_w_up),
        first(exp_w_down), norm_f_g,
        tm_in=512, tb_gmlp=256, tb_gla=512, tm_out=512, tb_comb=512, sb_rows=768)
    return out.reshape(b, t, d)
```

```python
import functools

import jax
import jax.numpy as jnp
from jax import lax
from jax.experimental import pallas as pl
from jax.experimental.pallas import tpu as pltpu

F32 = jnp.float32
BF16 = jnp.bfloat16
U32 = jnp.uint32
I32 = jnp.int32

D_MODEL = 4096
GMLP_WIDTH = 2048
GMLP_HEADS = 8
GMLP_HEAD_DIM = GMLP_WIDTH // GMLP_HEADS
GMLP_CHUNK = 128
GLA_WIDTH = 2048
GLA_HEADS = 4
GLA_KEY_WIDTH = 1024
GLA_HEAD_DK = GLA_KEY_WIDTH // GLA_HEADS
GLA_HEAD_DV = GLA_WIDTH // GLA_HEADS
GLA_GATE_RANK = 16
GLA_GATE_NORMALIZER = 16.0
GLA_CHUNK = 64
N_GROUPS = 8
EXPERTS_PER_GROUP = 8
N_EXPERTS = N_GROUPS * EXPERTS_PER_GROUP
D_EXPERT = D_MODEL // 4
NORM_EPS = 1e-6
D_PROJ_MAIN = 2 * GMLP_WIDTH + 2 * GLA_KEY_WIDTH + 2 * GLA_WIDTH

LANES = 128
HALF = D_MODEL // 2
ROW_TILES = HALF // LANES
FFN_KC = 1024
VMEM_LIMIT = 58 * 1024 * 1024

HIGHEST = lax.Precision.HIGHEST


def _params(semantics, vmem=VMEM_LIMIT):
    return pltpu.CompilerParams(dimension_semantics=semantics, vmem_limit_bytes=vmem)


def _inproj_body(x_ref, g_ref, w_ref, wlr_ref, z_ref, lr_ref, xn_ref):
    @pl.when(pl.program_id(1) == 0)
    def _():
        x = x_ref[...]
        ms = jnp.mean(x * x, axis=-1, keepdims=True)
        xn = ((x * lax.rsqrt(ms + NORM_EPS)) * g_ref[...]).astype(BF16)
        xn_ref[...] = xn
        lr_ref[...] = jnp.dot(xn, wlr_ref[...], preferred_element_type=F32)

    z_ref[...] = jnp.dot(xn_ref[...], w_ref[...], preferred_element_type=F32)


def _inproj(x, g, w_main, w_lr, *, tm, tn):
    t = x.shape[0]
    return pl.pallas_call(
        _inproj_body,
        grid=(t // tm, D_PROJ_MAIN // tn),
        in_specs=[
            pl.BlockSpec((tm, D_MODEL), lambda i, j: (i, 0)),
            pl.BlockSpec((1, D_MODEL), lambda i, j: (0, 0)),
            pl.BlockSpec((D_MODEL, tn), lambda i, j: (0, j)),
            pl.BlockSpec((D_MODEL, LANES), lambda i, j: (0, 0)),
        ],
        out_specs=[
            pl.BlockSpec((tm, tn), lambda i, j: (i, j)),
            pl.BlockSpec((tm, LANES), lambda i, j: (i, 0)),
        ],
        out_shape=[
            jax.ShapeDtypeStruct((t, D_PROJ_MAIN), F32),
            jax.ShapeDtypeStruct((t, LANES), F32),
        ],
        scratch_shapes=[pltpu.VMEM((tm, D_MODEL), BF16)],
        compiler_params=_params(("arbitrary", "arbitrary")),
        name="inproj",
    )(x, g, w_main, w_lr)


def _gelu(x):
    return x * (lax.erf(x * (2.0 ** -0.5)) + 1.0) * 0.5


def _gmlp_body(u_ref, v_ref, vg_ref, ws_ref, bst_ref, y_ref, *, n_chunks):
    row = lax.broadcasted_iota(I32, (GMLP_CHUNK, GMLP_CHUNK), 0)
    col = lax.broadcasted_iota(I32, (GMLP_CHUNK, GMLP_CHUNK), 1)
    causal = row >= col
    for c in range(n_chunks):
        rows = slice(c * GMLP_CHUNK, (c + 1) * GMLP_CHUNK)
        gv = _gelu(v_ref[rows, :])
        mu = jnp.mean(gv, axis=-1, keepdims=True)
        vc = gv - mu
        var = jnp.mean(vc * vc, axis=-1, keepdims=True)
        vn = ((vc * lax.rsqrt(var + NORM_EPS)) * vg_ref[...]).astype(BF16)
        for h in range(GMLP_HEADS):
            cols = slice(h * GMLP_HEAD_DIM, (h + 1) * GMLP_HEAD_DIM)
            wm = jnp.where(causal, ws_ref[h], 0.0).astype(BF16)
            s = jnp.dot(wm, vn[:, cols], preferred_element_type=F32) + bst_ref[:, h:h + 1]
            y_ref[rows, cols] = (_gelu(u_ref[rows, cols]) * s).astype(BF16)


def _gmlp(z, v_g, ws, bs_t, *, tb):
    t = z.shape[0]
    blk = GMLP_WIDTH
    return pl.pallas_call(
        functools.partial(_gmlp_body, n_chunks=tb // GMLP_CHUNK),
        grid=(t // tb,),
        in_specs=[
            pl.BlockSpec((tb, blk), lambda i: (i, 0)),
            pl.BlockSpec((tb, blk), lambda i: (i, 1)),
            pl.BlockSpec((1, blk), lambda i: (0, 0)),
            pl.BlockSpec((GMLP_HEADS, GMLP_CHUNK, GMLP_CHUNK), lambda i: (0, 0, 0)),
            pl.BlockSpec((GMLP_CHUNK, GMLP_HEADS), lambda i: (0, 0)),
        ],
        out_specs=pl.BlockSpec((tb, blk), lambda i: (i, 0)),
        out_shape=jax.ShapeDtypeStruct((t, blk), BF16),
        compiler_params=_params(("arbitrary",)),
        name="gmlp",
    )(z, z, v_g, ws, bs_t)


def _gla_body(q_ref, k_ref, v_ref, r_ref, lr_ref, wa2_ref, ba_ref, gn_ref, y_ref, st_ref, bcum_ref, *, n_chunks):
    @pl.when(pl.program_id(0) == 0)
    def _():
        st_ref[...] = jnp.zeros_like(st_ref)

    c_len = GLA_CHUNK
    tb = lr_ref.shape[0]
    tn_dims = (((0,), (0,)), ((), ()))
    nt_dims = (((1,), (1,)), ((), ()))

    logit = jnp.dot(lr_ref[...].astype(BF16), wa2_ref[...], preferred_element_type=F32) + ba_ref[...]
    bcum = (jnp.minimum(logit, 0.0) - jnp.log1p(jnp.exp(-jnp.abs(logit)))) * (1.0 / GLA_GATE_NORMALIZER)
    row_in_chunk = lax.broadcasted_iota(I32, (tb, 1), 0) & (c_len - 1)
    step = 1
    while step < c_len:
        bcum = bcum + jnp.where(row_in_chunk >= step, pltpu.roll(bcum, step, axis=0), 0.0)
        step *= 2
    bcum_ref[...] = bcum

    row = lax.broadcasted_iota(I32, (c_len, c_len), 0)
    col = lax.broadcasted_iota(I32, (c_len, c_len), 1)
    causal = row >= col

    def chunk(c, carry):
        t0 = pl.multiple_of(c * c_len, c_len)
        rows = pl.ds(t0, c_len)
        for h in range(GLA_HEADS):
            kc = slice(h * GLA_HEAD_DK, (h + 1) * GLA_HEAD_DK)
            vc = slice(h * GLA_HEAD_DV, (h + 1) * GLA_HEAD_DV)
            bcum = bcum_ref[rows, kc]
            b_last = bcum[c_len - 1:c_len, :]
            q = q_ref[rows, kc] * (GLA_HEAD_DK ** -0.5)
            k = k_ref[rows, kc]
            q_e = (q * jnp.exp(bcum)).astype(BF16)
            k_e = (k * jnp.exp(-bcum)).astype(BF16)
            k_end = (k * jnp.exp(b_last - bcum)).astype(BF16)
            v = v_ref[rows, vc].astype(BF16)
            scores = lax.dot_general(q_e, k_e, nt_dims, preferred_element_type=F32)
            scores = jnp.where(causal, scores, 0.0).astype(BF16)
            state_t = st_ref[h]
            o = jnp.dot(scores, v, preferred_element_type=F32)
            o = o + lax.dot_general(q_e, state_t.astype(BF16), nt_dims, preferred_element_type=F32)
            kv_t = lax.dot_general(v, k_end, tn_dims, preferred_element_type=F32)
            st_ref[h] = state_t * jnp.exp(b_last) + kv_t
            o = (o * lax.rsqrt(jnp.mean(o * o, axis=-1, keepdims=True) + NORM_EPS)) * gn_ref[...]
            r = r_ref[rows, vc]
            y_ref[rows, vc] = (o * (r * jax.nn.sigmoid(r))).astype(BF16)
        return carry

    lax.fori_loop(0, n_chunks, chunk, 0)


def _gla(z, lr, wa2p, ba, gn, *, tb):
    t = z.shape[0]
    kw, vw = GLA_KEY_WIDTH, GLA_WIDTH
    return pl.pallas_call(
        functools.partial(_gla_body, n_chunks=tb // GLA_CHUNK),
        grid=(t // tb,),
        in_specs=[
            pl.BlockSpec((tb, kw), lambda i: (i, 4)),
            pl.BlockSpec((tb, kw), lambda i: (i, 5)),
            pl.BlockSpec((tb, vw), lambda i: (i, 3)),
            pl.BlockSpec((tb, vw), lambda i: (i, 4)),
            pl.BlockSpec((tb, LANES), lambda i: (i, 0)),
            pl.BlockSpec((LANES, kw), lambda i: (0, 0)),
            pl.BlockSpec((1, kw), lambda i: (0, 0)),
            pl.BlockSpec((1, GLA_HEAD_DV), lambda i: (0, 0)),
        ],
        out_specs=pl.BlockSpec((tb, vw), lambda i: (i, 0)),
        out_shape=jax.ShapeDtypeStruct((t, vw), BF16),
        scratch_shapes=[pltpu.VMEM((GLA_HEADS, GLA_HEAD_DV, GLA_HEAD_DK), F32),
                        pltpu.VMEM((tb, kw), F32)],
        compiler_params=_params(("arbitrary",)),
        name="gla",
    )(z, z, z, z, lr, wa2p, ba, gn)


def _bf16_bits(x):
    return lax.bitcast_convert_type(x.astype(BF16).astype(F32), U32)


def _outproj_body(ya_ref, yb_ref, x_ref, woa_ref, wob_ref, g2_ref, wr_ref, br_ref,
                  h1_ref, hp_ref, gate_ref, eid_ref, hacc_ref, *, nj, tn):
    j = pl.program_id(1)
    tm = x_ref.shape[0]
    acc = jnp.dot(ya_ref[...], woa_ref[...], preferred_element_type=F32)
    acc = acc + jnp.dot(yb_ref[...], wob_ref[...], preferred_element_type=F32)
    h = x_ref[...] + acc
    h1_ref[...] = h
    hacc_ref[j] = h

    @pl.when(j == nj - 1)
    def _():
        ssq = jnp.sum(hacc_ref[0] * hacc_ref[0], axis=-1, keepdims=True)
        for jj in range(1, nj):
            ssq = ssq + jnp.sum(hacc_ref[jj] * hacc_ref[jj], axis=-1, keepdims=True)
        rstd = lax.rsqrt(ssq * (1.0 / D_MODEL) + NORM_EPS)
        logits = br_ref[...]

        def normed(jj):
            cols = slice(jj * tn, (jj + 1) * tn)
            hn = (hacc_ref[jj] * rstd) * g2_ref[:, cols]
            return hn, jnp.dot(hn.astype(BF16), wr_ref[cols, :], preferred_element_type=F32)

        kb = FFN_KC // tn
        for wb in range(nj // 2):
            q, hh = divmod(wb, kb)
            lo, d_lo = normed(2 * q * kb + hh)
            hi, d_hi = normed((2 * q + 1) * kb + hh)
            logits = logits + d_lo + d_hi
            packed = (_bf16_bits(hi) & jnp.uint32(0xFFFF0000)) | (_bf16_bits(lo) >> 16)
            for c in range(tn // LANES):
                ct = wb * (tn // LANES) + c
                hp_ref[pl.ds(ct, tm, stride=ROW_TILES), :] = packed[:, c * LANES:(c + 1) * LANES]

        lane = lax.broadcasted_iota(I32, logits.shape, 1)
        neg = jnp.float32(-jnp.inf)
        big = jnp.int32(LANES)
        cmask = lane < N_GROUPS
        cl = jnp.where(cmask, logits, neg)
        cmax = jnp.max(cl, axis=-1, keepdims=True)
        g_sel = jnp.min(jnp.where(cl == cmax, lane, big), axis=-1, keepdims=True)
        p_sel = 1.0 / jnp.sum(jnp.where(cmask, jnp.exp(logits - cmax), 0.0), axis=-1, keepdims=True)
        f0 = N_GROUPS + g_sel * EXPERTS_PER_GROUP
        fl = jnp.where((lane >= f0) & (lane < f0 + EXPERTS_PER_GROUP), logits, neg)
        v1 = jnp.max(fl, axis=-1, keepdims=True)
        i1 = jnp.min(jnp.where(fl == v1, lane, big), axis=-1, keepdims=True)
        fl2 = jnp.where(lane == i1, neg, fl)
        v2 = jnp.max(fl2, axis=-1, keepdims=True)
        i2 = jnp.min(jnp.where(fl2 == v2, lane, big), axis=-1, keepdims=True)
        e2 = jnp.exp(v2 - v1)
        w1 = p_sel / (1.0 + e2)
        w2 = p_sel * e2 / (1.0 + e2)
        gate_ref[...] = jnp.where(lane == 0, w1, jnp.where(lane == 1, w2, 0.0))
        eid_ref[...] = jnp.where(lane == 0, i1 - N_GROUPS, jnp.where(lane == 1, i2 - N_GROUPS, 0))


def _outproj(ya, yb, x, wo, g2, wr, br, *, tm, tn):
    t = x.shape[0]
    nj = D_MODEL // tn
    assert FFN_KC % tn == 0 and GMLP_WIDTH == GLA_WIDTH
    return pl.pallas_call(
        functools.partial(_outproj_body, nj=nj, tn=tn),
        grid=(t // tm, nj),
        in_specs=[
            pl.BlockSpec((tm, GMLP_WIDTH), lambda i, j: (i, 0)),
            pl.BlockSpec((tm, GLA_WIDTH), lambda i, j: (i, 0)),
            pl.BlockSpec((tm, tn), lambda i, j: (i, j)),
            pl.BlockSpec((GMLP_WIDTH, tn), lambda i, j: (0, j)),
            pl.BlockSpec((GLA_WIDTH, tn), lambda i, j: (1, j)),
            pl.BlockSpec((1, D_MODEL), lambda i, j: (0, 0)),
            pl.BlockSpec((D_MODEL, LANES), lambda i, j: (0, 0)),
            pl.BlockSpec((1, LANES), lambda i, j: (0, 0)),
        ],
        out_specs=[
            pl.BlockSpec((tm, tn), lambda i, j: (i, j)),
            pl.BlockSpec((tm * ROW_TILES, LANES), lambda i, j: (i, 0)),
            pl.BlockSpec((tm, LANES), lambda i, j: (i, 0)),
            pl.BlockSpec((tm, LANES), lambda i, j: (i, 0)),
        ],
        out_shape=[
            jax.ShapeDtypeStruct((t, D_MODEL), F32),
            jax.ShapeDtypeStruct((t * ROW_TILES, LANES), U32),
            jax.ShapeDtypeStruct((t, LANES), F32),
            jax.ShapeDtypeStruct((t, LANES), I32),
        ],
        scratch_shapes=[pltpu.VMEM((nj, tm, tn), F32)],
        compiler_params=_params(("arbitrary", "arbitrary")),
        name="outproj_router",
    )(ya, yb, x, wo, wo, g2, wr, br)


FFN_UP_CHUNKS = D_MODEL // FFN_KC
FFN_NC = 1024
FFN_DOWN_CHUNKS = D_MODEL // FFN_NC
FFN_CHUNKS = 2 * FFN_UP_CHUNKS + FFN_DOWN_CHUNKS
FFN_NBUF = 4
ROW_UNIT = 128
DMA_UNROLL = 8
assert FFN_CHUNKS % FFN_NBUF == 0 and FFN_KC == D_EXPERT == FFN_NC


def _pack_bf16_pair(lo, hi):
    return (_bf16_bits(hi) & jnp.uint32(0xFFFF0000)) | (_bf16_bits(lo) >> 16)


def _unpack_lo(w):
    return lax.bitcast_convert_type(w << 16, F32)


def _unpack_hi(w):
    return lax.bitcast_convert_type(w & jnp.uint32(0xFFFF0000), F32)


def _ffn_body(sbe_ref, sbr_ref, nsb_ref, tok0_ref, tokn_ref, hp_hbm, wg_hbm, wu_hbm, wd_hbm, ys_ref,
              g_ref, gsem, wbuf_ref, wsem, wb_ref, xk_ref, gacc_ref, uacc_ref, a_ref, *, rows):
    s = pl.program_id(0)
    nsb = nsb_ref[0]
    valid = s < nsb
    slot = lax.rem(s, 2)
    s_next = jnp.minimum(s + 1, sbr_ref.shape[0] - 1)
    word_tiles = FFN_KC // LANES

    def units(n_rows):
        return lax.shift_right_logical(n_rows + (ROW_UNIT - 1), ROW_UNIT.bit_length() - 1)

    n_units = units(sbr_ref[s])

    def row_copy(tok_ref, slot_, i):
        src = hp_hbm.at[pl.ds(pl.multiple_of(tok_ref[0, 0, i] * ROW_TILES, ROW_TILES), ROW_TILES)]
        dst = g_ref.at[slot_, pl.ds(pl.multiple_of(i * ROW_TILES, ROW_TILES), ROW_TILES)]
        return pltpu.make_async_copy(src, dst, gsem.at[slot_])

    def for_each_row(n_units_, fn):
        def unit_body(u, c):
            def body(i, c2):
                fn(u * ROW_UNIT + i)
                return c2
            return lax.fori_loop(0, ROW_UNIT, body, c, unroll=DMA_UNROLL)
        lax.fori_loop(0, n_units_, unit_body, 0)

    def for_row_blocks(fn):
        pair = 2 * ROW_UNIT
        for j in range(rows // pair):
            @pl.when(n_units >= 2 * j + 2)
            def _(j=j):
                fn(pair * j, pair)

            @pl.when(n_units == 2 * j + 1)
            def _(j=j):
                fn(pair * j, ROW_UNIT)

    def chunk_copy(e, ci):
        if ci < 2 * FFN_UP_CHUNKS:
            w_hbm = wg_hbm if ci % 2 == 0 else wu_hbm
            src = w_hbm.at[e, pl.ds((ci // 2) * FFN_KC, FFN_KC), :]
        else:
            src = wd_hbm.at[e, :, pl.ds((ci - 2 * FFN_UP_CHUNKS) * FFN_NC, FFN_NC)]
        b = ci % FFN_NBUF
        return pltpu.make_async_copy(src, wbuf_ref.at[b], wsem.at[b])

    @pl.when(s == 0)
    def _():
        for ci in range(FFN_NBUF):
            chunk_copy(sbe_ref[0], ci).start()
        for_each_row(units(sbr_ref[0]), lambda i: row_copy(tok0_ref, 0, i).start())

    @pl.when(valid)
    def _():
        e_cur = sbe_ref[s]
        has_next = s + 1 < nsb
        for_each_row(n_units, lambda i: row_copy(tok0_ref, slot, i).wait())

        @pl.when(has_next)
        def _():
            for_each_row(units(sbr_ref[s_next]), lambda i: row_copy(tokn_ref, 1 - slot, i).start())

        for ci in range(FFN_CHUNKS):
            chunk_copy(e_cur, ci).wait()
            wb = wb_ref.at[ci % 2]
            wb[...] = wbuf_ref[ci % FFN_NBUF].astype(BF16)
            ahead = ci + FFN_NBUF
            if ahead < FFN_CHUNKS:
                chunk_copy(e_cur, ahead).start()
            else:
                @pl.when(has_next)
                def _(ahead=ahead):
                    chunk_copy(sbe_ref[s_next], ahead - FFN_CHUNKS).start()

            if ci < 2 * FFN_UP_CHUNKS:
                k, is_up = divmod(ci, 2)

                def block(r0, m, k=k, is_up=is_up, wb=wb):
                    r = slice(r0, r0 + m)
                    if not is_up and k % 2 == 0:
                        first = r0 * ROW_TILES + (k // 2) * word_tiles
                        w = jnp.concatenate(
                            [g_ref[slot, pl.ds(first + c, m, stride=ROW_TILES), :] for c in range(word_tiles)],
                            axis=1)
                        xk_ref[0, r, :] = _unpack_lo(w).astype(BF16)
                        xk_ref[1, r, :] = _unpack_hi(w).astype(BF16)
                    acc_ref = uacc_ref if is_up else gacc_ref
                    d = jnp.dot(xk_ref[k % 2, r, :], wb[...], preferred_element_type=F32)
                    if k > 0:
                        d = acc_ref[r, :] + d
                    if is_up and k == FFN_UP_CHUNKS - 1:
                        g = gacc_ref[r, :]
                        a_ref[r, :] = ((g * jax.nn.sigmoid(g)) * d).astype(BF16)
                    else:
                        acc_ref[r, :] = d
            else:
                n = ci - 2 * FFN_UP_CHUNKS

                def block(r0, m, n=n, wb=wb):
                    y = jnp.dot(a_ref[r0:r0 + m, :], wb[...], preferred_element_type=F32)
                    half = FFN_NC // 2
                    ys_ref[r0:r0 + m, n * half:(n + 1) * half] = _pack_bf16_pair(y[:, :half], y[:, half:])

            for_row_blocks(block)

        for u in range(rows // ROW_UNIT):
            @pl.when(n_units <= u)
            def _(u=u):
                ys_ref[u * ROW_UNIT:(u + 1) * ROW_UNIT, :] = jnp.zeros((ROW_UNIT, ys_ref.shape[1]), U32)

    @pl.when(jnp.logical_not(valid))
    def _():
        ys_ref[...] = jnp.zeros_like(ys_ref)


def _ffn(sb_e, sb_rows, nsb, tok3, hp, w_gate, w_up, w_down, *, rows):
    s_max = sb_e.shape[0]
    assert rows % (2 * ROW_UNIT) == 0
    return pl.pallas_call(
        functools.partial(_ffn_body, rows=rows),
        grid_spec=pltpu.PrefetchScalarGridSpec(
            num_scalar_prefetch=3,
            grid=(s_max,),
            in_specs=[
                pl.BlockSpec((1, 1, rows), lambda s, sbe, sbr, nsb: (0, 0, 0), memory_space=pltpu.SMEM),
                pl.BlockSpec((1, 1, rows), lambda s, sbe, sbr, nsb: (jnp.minimum(s + 1, s_max - 1), 0, 0),
                             memory_space=pltpu.SMEM),
                pl.BlockSpec(memory_space=pl.ANY),
                pl.BlockSpec(memory_space=pl.ANY),
                pl.BlockSpec(memory_space=pl.ANY),
                pl.BlockSpec(memory_space=pl.ANY),
            ],
            out_specs=pl.BlockSpec((rows, HALF), lambda s, sbe, sbr, nsb: (s, 0)),
            scratch_shapes=[
                pltpu.VMEM((2, rows * ROW_TILES, LANES), U32),
                pltpu.SemaphoreType.DMA((2,)),
                pltpu.VMEM((FFN_NBUF, FFN_KC, D_EXPERT), F32),
                pltpu.SemaphoreType.DMA((FFN_NBUF,)),
                pltpu.VMEM((2, FFN_KC, D_EXPERT), BF16),
                pltpu.VMEM((2, rows, FFN_KC), BF16),
                pltpu.VMEM((rows, D_EXPERT), F32),
                pltpu.VMEM((rows, D_EXPERT), F32),
                pltpu.VMEM((rows, D_EXPERT), BF16),
            ],
        ),
        out_shape=jax.ShapeDtypeStruct((s_max * rows, HALF), U32),
        compiler_params=_params(("arbitrary",)),
        name="moe_ffn",
    )(sb_e, sb_rows, nsb, tok3, tok3, hp, w_gate, w_up, w_down)


def _combine_body(pos0_ref, posn_ref, h1_ref, gate_ref, gf_ref, ys_hbm, out_ref, buf_ref, sem, *, tb):
    i = pl.program_id(0)
    slot = lax.rem(i, 2)

    def row_copy(pos_ref, slot_, k, j):
        src = ys_hbm.at[pl.ds(pos_ref[0, k, j], 1)]
        return pltpu.make_async_copy(src, buf_ref.at[slot_, k, pl.ds(j, 1)], sem.at[slot_])

    def for_each_row(fn):
        def body(j, c):
            fn(0, j)
            fn(1, j)
            return c
        lax.fori_loop(0, tb, body, 0, unroll=DMA_UNROLL)

    @pl.when(i == 0)
    def _():
        for_each_row(lambda k, j: row_copy(pos0_ref, 0, k, j).start())

    for_each_row(lambda k, j: row_copy(pos0_ref, slot, k, j).wait())

    @pl.when(i + 1 < pl.num_programs(0))
    def _():
        for_each_row(lambda k, j: row_copy(posn_ref, 1 - slot, k, j).start())

    gate = gate_ref[...]
    half = FFN_NC // 2
    for n in range(FFN_DOWN_CHUNKS):
        w0 = buf_ref[slot, 0, :, n * half:(n + 1) * half]
        w1 = buf_ref[slot, 1, :, n * half:(n + 1) * half]
        for unpack, c0 in ((_unpack_lo, n * FFN_NC), (_unpack_hi, n * FFN_NC + half)):
            cols = slice(c0, c0 + half)
            y = unpack(w0) * gate[:, 0:1] + unpack(w1) * gate[:, 1:2]
            out_ref[:, cols] = h1_ref[:, cols] + y
    h = out_ref[...]
    ms = jnp.mean(h * h, axis=-1, keepdims=True)
    out_ref[...] = (h * lax.rsqrt(ms + NORM_EPS)) * gf_ref[...]


def _combine(pos3, h1, gate, gf, ys, *, tb):
    t = h1.shape[0]
    n_blocks = t // tb
    return pl.pallas_call(
        functools.partial(_combine_body, tb=tb),
        grid=(n_blocks,),
        in_specs=[
            pl.BlockSpec((1, 2, tb), lambda i: (0, 0, 0), memory_space=pltpu.SMEM),
            pl.BlockSpec((1, 2, tb), lambda i: (jnp.minimum(i + 1, n_blocks - 1), 0, 0), memory_space=pltpu.SMEM),
            pl.BlockSpec((tb, D_MODEL), lambda i: (i, 0)),
            pl.BlockSpec((tb, LANES), lambda i: (i, 0)),
            pl.BlockSpec((1, D_MODEL), lambda i: (0, 0)),
            pl.BlockSpec(memory_space=pl.ANY),
        ],
        out_specs=pl.BlockSpec((tb, D_MODEL), lambda i: (i, 0)),
        out_shape=jax.ShapeDtypeStruct((t, D_MODEL), F32),
        scratch_shapes=[pltpu.VMEM((2, 2, tb, HALF), U32), pltpu.SemaphoreType.DMA((2,))],
        compiler_params=_params(("arbitrary",)),
        name="moe_combine",
    )(pos3, pos3, h1, gate, gf, ys)


def _rank_body(eid_ref, rank_ref, counts_ref, carry_ref):
    @pl.when(pl.program_id(0) == 0)
    def _():
        carry_ref[...] = jnp.zeros_like(carry_ref)

    tb = eid_ref.shape[0]
    eid = eid_ref[...]
    lane = lax.broadcasted_iota(I32, eid.shape, 1)
    oh0 = lane == eid[:, 0:1]
    oh1 = lane == eid[:, 1:2]
    hits = jnp.where(oh0 | oh1, 1.0, 0.0)
    row = lax.broadcasted_iota(I32, (tb, tb), 0)
    col = lax.broadcasted_iota(I32, (tb, tb), 1)
    earlier = jnp.where(col < row, 1.0, 0.0).astype(BF16)
    before = jnp.dot(earlier, hits.astype(BF16), preferred_element_type=F32) + carry_ref[...]
    r0 = jnp.sum(jnp.where(oh0, before, 0.0), axis=-1, keepdims=True)
    r1 = jnp.sum(jnp.where(oh1, before, 0.0), axis=-1, keepdims=True)
    rank_ref[...] = jnp.where(lane == 0, r0, jnp.where(lane == 1, r1, 0.0)).astype(I32)
    carry_ref[...] = carry_ref[...] + jnp.sum(hits, axis=0, keepdims=True)
    counts_ref[...] = carry_ref[...].astype(I32)


def _rank(eid, *, tb):
    t = eid.shape[0]
    return pl.pallas_call(
        _rank_body,
        grid=(t // tb,),
        in_specs=[pl.BlockSpec((tb, LANES), lambda i: (i, 0))],
        out_specs=[pl.BlockSpec((tb, LANES), lambda i: (i, 0)), pl.BlockSpec((1, LANES), lambda i: (0, 0))],
        out_shape=[jax.ShapeDtypeStruct((t, LANES), I32), jax.ShapeDtypeStruct((1, LANES), I32)],
        scratch_shapes=[pltpu.VMEM((1, LANES), F32)],
        compiler_params=_params(("arbitrary",)),
        name="moe_rank",
    )(eid)


def _dispatch_plan(eid, *, rows, s_max):
    t = eid.shape[0]
    rank2, counts_row = _rank(eid, tb=512)
    flat_e = eid[:, :2].reshape(-1)
    rank = rank2[:, :2].reshape(-1)
    counts = counts_row[0, :N_EXPERTS]
    nsb_e = (counts + rows - 1) // rows
    sb_end = jnp.cumsum(nsb_e)
    sb_start = sb_end - nsb_e
    nsb = sb_end[-1]
    pos = sb_start[flat_e] * rows + rank
    s_idx = jnp.minimum(jnp.arange(s_max, dtype=I32), nsb - 1)
    sb_e = jnp.minimum(jnp.searchsorted(sb_end, s_idx, side="right"), N_EXPERTS - 1).astype(I32)
    sb_rows = jnp.clip(counts[sb_e] - (s_idx - sb_start[sb_e]) * rows, 0, rows)
    sb_rows = jnp.where(jnp.arange(s_max) < nsb, sb_rows, 0).astype(I32)
    flat_tok = jnp.arange(2 * t, dtype=I32) // 2
    tok = jnp.zeros((s_max * rows,), I32).at[pos].set(flat_tok)
    return sb_e, sb_rows, nsb.reshape(1).astype(I32), tok.reshape(s_max, 1, rows), pos.reshape(t, 2)


def _layer(x, norm1_g, w_in, gmlp_v_g, gmlp_ws, gmlp_bs, gla_wa2, gla_ba, gla_norm_g, w_out, norm2_g,
           router_coarse_w, router_coarse_b, router_fine_w, router_fine_b, exp_w_gate, exp_w_up, exp_w_down,
           norm_f_g, *, tm_in, tb_gmlp, tb_gla, tm_out, tb_comb, sb_rows):
    t = x.shape[0]
    w_main = w_in.astype(BF16)
    w_lr = jnp.pad(w_in[:, D_PROJ_MAIN:], ((0, 0), (0, LANES - GLA_GATE_RANK))).astype(BF16)
    wa2p = jnp.pad(gla_wa2, ((0, LANES - GLA_GATE_RANK), (0, 0))).astype(BF16)
    wo = w_out.astype(BF16)
    n_route = N_GROUPS + N_EXPERTS
    wr = jnp.pad(jnp.concatenate([router_coarse_w, router_fine_w], axis=1),
                 ((0, 0), (0, LANES - n_route))).astype(BF16)
    br = jnp.pad(jnp.concatenate([router_coarse_b, router_fine_b]), (0, LANES - n_route)).reshape(1, LANES)

    z, lr = _inproj(x, norm1_g.reshape(1, -1), w_main, w_lr, tm=tm_in, tn=1024)
    y_a = _gmlp(z, gmlp_v_g.reshape(1, -1), gmlp_ws, gmlp_bs.T, tb=tb_gmlp)
    y_b = _gla(z, lr, wa2p, gla_ba.reshape(1, -1), gla_norm_g.reshape(1, -1), tb=tb_gla)
    h1, hp, gate, eid = _outproj(y_a, y_b, x, wo, norm2_g.reshape(1, -1), wr, br, tm=tm_out, tn=512)

    s_max = N_EXPERTS + (2 * t) // sb_rows
    sb_e, sb_n, nsb, tok3, pos = _dispatch_plan(eid, rows=sb_rows, s_max=s_max)
    ys = _ffn(sb_e, sb_n, nsb, tok3, hp, exp_w_gate, exp_w_up, exp_w_down, rows=sb_rows)
    pos3 = pos.reshape(t // tb_comb, tb_comb, 2).transpose(0, 2, 1)
    return _combine(pos3, h1, gate, norm_f_g.reshape(1, -1), ys, tb=tb_comb)


def kernel(x, norm1_g, w_in, gmlp_v_g, gmlp_ws, gmlp_bs, gla_wa2, gla_ba, gla_norm_g, w_out, norm2_g,
           router_coarse_w, router_coarse_b, router_fine_w, router_fine_b, exp_w_gate, exp_w_up, exp_w_down,
           norm_f_g):
    b, t, d = x.shape
    assert b == 1 and d == D_MODEL and norm1_g.shape[0] == 1, "one sequence, one layer, the stated widths"

    def first(a):
        return a.reshape(a.shape[1:])

    out = _layer(
        first(x), first(norm1_g), first(w_in), first(gmlp_v_g), first(gmlp_ws), first(gmlp_bs), first(gla_wa2),
        first(gla_ba), first(gla_norm_g), first(w_out), first(norm2_g), first(router_coarse_w),
        first(router_coarse_b), first(router_fine_w), first(router_fine_b), first(exp_w_gate), first(exp_w_up),
        first(exp_w_down), norm_f_g,
        tm_in=512, tb_gmlp=256, tb_gla=512, tm_out=512, tb_comb=512, sb_rows=768)
    return out.reshape(b, t, d)
```

```python
import functools

import jax
import jax.numpy as jnp
from jax import lax
from jax.experimental import pallas as pl
from jax.experimental.pallas import tpu as pltpu

F32 = jnp.float32
BF16 = jnp.bfloat16
U32 = jnp.uint32
I32 = jnp.int32

D_MODEL = 4096
GMLP_WIDTH = 2048
GMLP_HEADS = 8
GMLP_HEAD_DIM = GMLP_WIDTH // GMLP_HEADS
GMLP_CHUNK = 128
GLA_WIDTH = 2048
GLA_HEADS = 4
GLA_KEY_WIDTH = 1024
GLA_HEAD_DK = GLA_KEY_WIDTH // GLA_HEADS
GLA_HEAD_DV = GLA_WIDTH // GLA_HEADS
GLA_GATE_RANK = 16
GLA_GATE_NORMALIZER = 16.0
GLA_CHUNK = 64
N_GROUPS = 8
EXPERTS_PER_GROUP = 8
N_EXPERTS = N_GROUPS * EXPERTS_PER_GROUP
D_EXPERT = D_MODEL // 4
NORM_EPS = 1e-6
D_PROJ_MAIN = 2 * GMLP_WIDTH + 2 * GLA_KEY_WIDTH + 2 * GLA_WIDTH

LANES = 128
HALF = D_MODEL // 2
ROW_TILES = HALF // LANES
FFN_KC = 1024
VMEM_LIMIT = 58 * 1024 * 1024

HIGHEST = lax.Precision.HIGHEST


def _params(semantics, vmem=VMEM_LIMIT):
    return pltpu.CompilerParams(dimension_semantics=semantics, vmem_limit_bytes=vmem)


def _cast_body(w_ref, o_ref):
    o_ref[...] = w_ref[...].astype(BF16)


def _to_bf16(w, *, tr):
    r, c = w.shape
    return pl.pallas_call(
        _cast_body,
        grid=(r // tr,),
        in_specs=[pl.BlockSpec((tr, c), lambda i: (i, 0))],
        out_specs=pl.BlockSpec((tr, c), lambda i: (i, 0)),
        out_shape=jax.ShapeDtypeStruct((r, c), BF16),
        compiler_params=_params(("arbitrary",)),
        name="to_bf16",
    )(w)


def _inproj_body(x_ref, g_ref, w_ref, wlr_ref, z_ref, lr_ref, xn_ref):
    @pl.when(pl.program_id(1) == 0)
    def _():
        x = x_ref[...]
        ms = jnp.mean(x * x, axis=-1, keepdims=True)
        xn = ((x * lax.rsqrt(ms + NORM_EPS)) * g_ref[...]).astype(BF16)
        xn_ref[...] = xn
        lr_ref[...] = jnp.dot(xn, wlr_ref[...], preferred_element_type=F32)

    z_ref[...] = jnp.dot(xn_ref[...], w_ref[...], preferred_element_type=F32)


def _inproj(x, g, w_main, w_lr, *, tm, tn):
    t = x.shape[0]
    return pl.pallas_call(
        _inproj_body,
        grid=(t // tm, D_PROJ_MAIN // tn),
        in_specs=[
            pl.BlockSpec((tm, D_MODEL), lambda i, j: (i, 0)),
            pl.BlockSpec((1, D_MODEL), lambda i, j: (0, 0)),
            pl.BlockSpec((D_MODEL, tn), lambda i, j: (0, j)),
            pl.BlockSpec((D_MODEL, LANES), lambda i, j: (0, 0)),
        ],
        out_specs=[
            pl.BlockSpec((tm, tn), lambda i, j: (i, j)),
            pl.BlockSpec((tm, LANES), lambda i, j: (i, 0)),
        ],
        out_shape=[
            jax.ShapeDtypeStruct((t, D_PROJ_MAIN), F32),
            jax.ShapeDtypeStruct((t, LANES), F32),
        ],
        scratch_shapes=[pltpu.VMEM((tm, D_MODEL), BF16)],
        compiler_params=_params(("arbitrary", "arbitrary")),
        name="inproj",
    )(x, g, w_main, w_lr)


def _gelu(x):
    return x * (lax.erf(x * (2.0 ** -0.5)) + 1.0) * 0.5


def _gmlp_body(u_ref, v_ref, vg_ref, ws_ref, bst_ref, y_ref, *, n_chunks):
    row = lax.broadcasted_iota(I32, (GMLP_CHUNK, GMLP_CHUNK), 0)
    col = lax.broadcasted_iota(I32, (GMLP_CHUNK, GMLP_CHUNK), 1)
    causal = row >= col
    for c in range(n_chunks):
        rows = slice(c * GMLP_CHUNK, (c + 1) * GMLP_CHUNK)
        gv = _gelu(v_ref[rows, :])
        mu = jnp.mean(gv, axis=-1, keepdims=True)
        vc = gv - mu
        var = jnp.mean(vc * vc, axis=-1, keepdims=True)
        vn = ((vc * lax.rsqrt(var + NORM_EPS)) * vg_ref[...]).astype(BF16)
        for h in range(GMLP_HEADS):
            cols = slice(h * GMLP_HEAD_DIM, (h + 1) * GMLP_HEAD_DIM)
            wm = jnp.where(causal, ws_ref[h], 0.0).astype(BF16)
            s = jnp.dot(wm, vn[:, cols], preferred_element_type=F32) + bst_ref[:, h:h + 1]
            y_ref[rows, cols] = (_gelu(u_ref[rows, cols]) * s).astype(BF16)


def _gmlp(z, v_g, ws, bs_t, *, tb):
    t = z.shape[0]
    blk = GMLP_WIDTH
    return pl.pallas_call(
        functools.partial(_gmlp_body, n_chunks=tb // GMLP_CHUNK),
        grid=(t // tb,),
        in_specs=[
            pl.BlockSpec((tb, blk), lambda i: (i, 0)),
            pl.BlockSpec((tb, blk), lambda i: (i, 1)),
            pl.BlockSpec((1, blk), lambda i: (0, 0)),
            pl.BlockSpec((GMLP_HEADS, GMLP_CHUNK, GMLP_CHUNK), lambda i: (0, 0, 0)),
            pl.BlockSpec((GMLP_CHUNK, GMLP_HEADS), lambda i: (0, 0)),
        ],
        out_specs=pl.BlockSpec((tb, blk), lambda i: (i, 0)),
        out_shape=jax.ShapeDtypeStruct((t, blk), BF16),
        compiler_params=_params(("arbitrary",)),
        name="gmlp",
    )(z, z, v_g, ws, bs_t)


def _gla_body(q_ref, k_ref, v_ref, r_ref, lr_ref, wa2_ref, ba_ref, gn_ref, y_ref, st_ref, bcum_ref, *, n_chunks):
    @pl.when(pl.program_id(0) == 0)
    def _():
        st_ref[...] = jnp.zeros_like(st_ref)

    c_len = GLA_CHUNK
    tb = lr_ref.shape[0]
    tn_dims = (((0,), (0,)), ((), ()))
    nt_dims = (((1,), (1,)), ((), ()))

    logit = jnp.dot(lr_ref[...].astype(BF16), wa2_ref[...], preferred_element_type=F32) + ba_ref[...]
    bcum = (jnp.minimum(logit, 0.0) - jnp.log1p(jnp.exp(-jnp.abs(logit)))) * (1.0 / GLA_GATE_NORMALIZER)
    row_in_chunk = lax.broadcasted_iota(I32, (tb, 1), 0) & (c_len - 1)
    step = 1
    while step < c_len:
        bcum = bcum + jnp.where(row_in_chunk >= step, pltpu.roll(bcum, step, axis=0), 0.0)
        step *= 2
    bcum_ref[...] = bcum

    row = lax.broadcasted_iota(I32, (c_len, c_len), 0)
    col = lax.broadcasted_iota(I32, (c_len, c_len), 1)
    causal = row >= col

    def chunk(c, carry):
        t0 = pl.multiple_of(c * c_len, c_len)
        rows = pl.ds(t0, c_len)
        for h in range(GLA_HEADS):
            kc = slice(h * GLA_HEAD_DK, (h + 1) * GLA_HEAD_DK)
            vc = slice(h * GLA_HEAD_DV, (h + 1) * GLA_HEAD_DV)
            bcum = bcum_ref[rows, kc]
            b_last = bcum[c_len - 1:c_len, :]
            q = q_ref[rows, kc] * (GLA_HEAD_DK ** -0.5)
            k = k_ref[rows, kc]
            q_e = (q * jnp.exp(bcum)).astype(BF16)
            k_e = (k * jnp.exp(-bcum)).astype(BF16)
            k_end = (k * jnp.exp(b_last - bcum)).astype(BF16)
            v = v_ref[rows, vc].astype(BF16)
            scores = lax.dot_general(q_e, k_e, nt_dims, preferred_element_type=F32)
            scores = jnp.where(causal, scores, 0.0).astype(BF16)
            state_t = st_ref[h]
            o = jnp.dot(scores, v, preferred_element_type=F32)
            o = o + lax.dot_general(q_e, state_t.astype(BF16), nt_dims, preferred_element_type=F32)
            kv_t = lax.dot_general(v, k_end, tn_dims, preferred_element_type=F32)
            st_ref[h] = state_t * jnp.exp(b_last) + kv_t
            o = (o * lax.rsqrt(jnp.mean(o * o, axis=-1, keepdims=True) + NORM_EPS)) * gn_ref[...]
            r = r_ref[rows, vc]
            y_ref[rows, vc] = (o * (r * jax.nn.sigmoid(r))).astype(BF16)
        return carry

    lax.fori_loop(0, n_chunks, chunk, 0)


def _gla(z, lr, wa2p, ba, gn, *, tb):
    t = z.shape[0]
    kw, vw = GLA_KEY_WIDTH, GLA_WIDTH
    return pl.pallas_call(
        functools.partial(_gla_body, n_chunks=tb // GLA_CHUNK),
        grid=(t // tb,),
        in_specs=[
            pl.BlockSpec((tb, kw), lambda i: (i, 4)),
            pl.BlockSpec((tb, kw), lambda i: (i, 5)),
            pl.BlockSpec((tb, vw), lambda i: (i, 3)),
            pl.BlockSpec((tb, vw), lambda i: (i, 4)),
            pl.BlockSpec((tb, LANES), lambda i: (i, 0)),
            pl.BlockSpec((LANES, kw), lambda i: (0, 0)),
            pl.BlockSpec((1, kw), lambda i: (0, 0)),
            pl.BlockSpec((1, GLA_HEAD_DV), lambda i: (0, 0)),
        ],
        out_specs=pl.BlockSpec((tb, vw), lambda i: (i, 0)),
        out_shape=jax.ShapeDtypeStruct((t, vw), BF16),
        scratch_shapes=[pltpu.VMEM((GLA_HEADS, GLA_HEAD_DV, GLA_HEAD_DK), F32),
                        pltpu.VMEM((tb, kw), F32)],
        compiler_params=_params(("arbitrary",)),
        name="gla",
    )(z, z, z, z, lr, wa2p, ba, gn)


def _bf16_bits(x):
    return lax.bitcast_convert_type(x.astype(BF16).astype(F32), U32)


def _outproj_body(ya_ref, yb_ref, x_ref, woa_ref, wob_ref, g2_ref, wr_ref, br_ref,
                  h1_ref, hp_ref, gate_ref, eid_ref, hacc_ref, *, nj, tn):
    j = pl.program_id(1)
    tm = x_ref.shape[0]
    acc = jnp.dot(ya_ref[...], woa_ref[...], preferred_element_type=F32)
    acc = acc + jnp.dot(yb_ref[...], wob_ref[...], preferred_element_type=F32)
    h = x_ref[...] + acc
    h1_ref[...] = h
    hacc_ref[j] = h

    @pl.when(j == nj - 1)
    def _():
        ssq = jnp.sum(hacc_ref[0] * hacc_ref[0], axis=-1, keepdims=True)
        for jj in range(1, nj):
            ssq = ssq + jnp.sum(hacc_ref[jj] * hacc_ref[jj], axis=-1, keepdims=True)
        rstd = lax.rsqrt(ssq * (1.0 / D_MODEL) + NORM_EPS)
        logits = br_ref[...]

        def normed(jj):
            cols = slice(jj * tn, (jj + 1) * tn)
            hn = (hacc_ref[jj] * rstd) * g2_ref[:, cols]
            return hn, jnp.dot(hn.astype(BF16), wr_ref[cols, :], preferred_element_type=F32)

        kb = FFN_KC // tn
        for wb in range(nj // 2):
            q, hh = divmod(wb, kb)
            lo, d_lo = normed(2 * q * kb + hh)
            hi, d_hi = normed((2 * q + 1) * kb + hh)
            logits = logits + d_lo + d_hi
            packed = (_bf16_bits(hi) & jnp.uint32(0xFFFF0000)) | (_bf16_bits(lo) >> 16)
            for c in range(tn // LANES):
                ct = wb * (tn // LANES) + c
                hp_ref[pl.ds(ct, tm, stride=ROW_TILES), :] = packed[:, c * LANES:(c + 1) * LANES]

        lane = lax.broadcasted_iota(I32, logits.shape, 1)
        neg = jnp.float32(-jnp.inf)
        big = jnp.int32(LANES)
        cmask = lane < N_GROUPS
        cl = jnp.where(cmask, logits, neg)
        cmax = jnp.max(cl, axis=-1, keepdims=True)
        g_sel = jnp.min(jnp.where(cl == cmax, lane, big), axis=-1, keepdims=True)
        p_sel = 1.0 / jnp.sum(jnp.where(cmask, jnp.exp(logits - cmax), 0.0), axis=-1, keepdims=True)
        f0 = N_GROUPS + g_sel * EXPERTS_PER_GROUP
        fl = jnp.where((lane >= f0) & (lane < f0 + EXPERTS_PER_GROUP), logits, neg)
        v1 = jnp.max(fl, axis=-1, keepdims=True)
        i1 = jnp.min(jnp.where(fl == v1, lane, big), axis=-1, keepdims=True)
        fl2 = jnp.where(lane == i1, neg, fl)
        v2 = jnp.max(fl2, axis=-1, keepdims=True)
        i2 = jnp.min(jnp.where(fl2 == v2, lane, big), axis=-1, keepdims=True)
        e2 = jnp.exp(v2 - v1)
        w1 = p_sel / (1.0 + e2)
        w2 = p_sel * e2 / (1.0 + e2)
        gate_ref[...] = jnp.where(lane == 0, w1, jnp.where(lane == 1, w2, 0.0))
        eid_ref[...] = jnp.where(lane == 0, i1 - N_GROUPS, jnp.where(lane == 1, i2 - N_GROUPS, 0))


def _outproj(ya, yb, x, wo, g2, wr, br, *, tm, tn):
    t = x.shape[0]
    nj = D_MODEL // tn
    assert FFN_KC % tn == 0 and GMLP_WIDTH == GLA_WIDTH
    return pl.pallas_call(
        functools.partial(_outproj_body, nj=nj, tn=tn),
        grid=(t // tm, nj),
        in_specs=[
            pl.BlockSpec((tm, GMLP_WIDTH), lambda i, j: (i, 0)),
            pl.BlockSpec((tm, GLA_WIDTH), lambda i, j: (i, 0)),
            pl.BlockSpec((tm, tn), lambda i, j: (i, j)),
            pl.BlockSpec((GMLP_WIDTH, tn), lambda i, j: (0, j)),
            pl.BlockSpec((GLA_WIDTH, tn), lambda i, j: (1, j)),
            pl.BlockSpec((1, D_MODEL), lambda i, j: (0, 0)),
            pl.BlockSpec((D_MODEL, LANES), lambda i, j: (0, 0)),
            pl.BlockSpec((1, LANES), lambda i, j: (0, 0)),
        ],
        out_specs=[
            pl.BlockSpec((tm, tn), lambda i, j: (i, j)),
            pl.BlockSpec((tm * ROW_TILES, LANES), lambda i, j: (i, 0)),
            pl.BlockSpec((tm, LANES), lambda i, j: (i, 0)),
            pl.BlockSpec((tm, LANES), lambda i, j: (i, 0)),
        ],
        out_shape=[
            jax.ShapeDtypeStruct((t, D_MODEL), F32),
            jax.ShapeDtypeStruct((t * ROW_TILES, LANES), U32),
            jax.ShapeDtypeStruct((t, LANES), F32),
            jax.ShapeDtypeStruct((t, LANES), I32),
        ],
        scratch_shapes=[pltpu.VMEM((nj, tm, tn), F32)],
        compiler_params=_params(("arbitrary", "arbitrary")),
        name="outproj_router",
    )(ya, yb, x, wo, wo, g2, wr, br)


FFN_UP_CHUNKS = D_MODEL // FFN_KC
FFN_NC = 1024
FFN_DOWN_CHUNKS = D_MODEL // FFN_NC
FFN_CHUNKS = 2 * FFN_UP_CHUNKS + FFN_DOWN_CHUNKS
FFN_NBUF = 4
ROW_UNIT = 128
DMA_UNROLL = 8
assert FFN_CHUNKS % FFN_NBUF == 0 and FFN_KC == D_EXPERT == FFN_NC


def _pack_bf16_pair(lo, hi):
    return (_bf16_bits(hi) & jnp.uint32(0xFFFF0000)) | (_bf16_bits(lo) >> 16)


def _unpack_lo(w):
    return lax.bitcast_convert_type(w << 16, F32)


def _unpack_hi(w):
    return lax.bitcast_convert_type(w & jnp.uint32(0xFFFF0000), F32)


def _ffn_body(sbe_ref, sbr_ref, nsb_ref, tok0_ref, tokn_ref, hp_hbm, wg_hbm, wu_hbm, wd_hbm, ys_ref,
              g_ref, gsem, wbuf_ref, wsem, wb_ref, xk_ref, gacc_ref, uacc_ref, a_ref, *, rows):
    s = pl.program_id(0)
    nsb = nsb_ref[0]
    valid = s < nsb
    slot = lax.rem(s, 2)
    s_next = jnp.minimum(s + 1, sbr_ref.shape[0] - 1)
    word_tiles = FFN_KC // LANES

    def units(n_rows):
        return lax.shift_right_logical(n_rows + (ROW_UNIT - 1), ROW_UNIT.bit_length() - 1)

    n_units = units(sbr_ref[s])

    def row_copy(tok_ref, slot_, i):
        src = hp_hbm.at[pl.ds(pl.multiple_of(tok_ref[0, 0, i] * ROW_TILES, ROW_TILES), ROW_TILES)]
        dst = g_ref.at[slot_, pl.ds(pl.multiple_of(i * ROW_TILES, ROW_TILES), ROW_TILES)]
        return pltpu.make_async_copy(src, dst, gsem.at[slot_])

    def for_each_row(n_units_, fn):
        def unit_body(u, c):
            def body(i, c2):
                fn(u * ROW_UNIT + i)
                return c2
            return lax.fori_loop(0, ROW_UNIT, body, c, unroll=DMA_UNROLL)
        lax.fori_loop(0, n_units_, unit_body, 0)

    def for_row_blocks(fn):
        pair = 2 * ROW_UNIT
        for j in range(rows // pair):
            @pl.when(n_units >= 2 * j + 2)
            def _(j=j):
                fn(pair * j, pair)

            @pl.when(n_units == 2 * j + 1)
            def _(j=j):
                fn(pair * j, ROW_UNIT)

    def chunk_copy(e, ci):
        if ci < 2 * FFN_UP_CHUNKS:
            w_hbm = wg_hbm if ci % 2 == 0 else wu_hbm
            src = w_hbm.at[e, pl.ds((ci // 2) * FFN_KC, FFN_KC), :]
        else:
            src = wd_hbm.at[e, :, pl.ds((ci - 2 * FFN_UP_CHUNKS) * FFN_NC, FFN_NC)]
        b = ci % FFN_NBUF
        return pltpu.make_async_copy(src, wbuf_ref.at[b], wsem.at[b])

    @pl.when(s == 0)
    def _():
        for ci in range(FFN_NBUF):
            chunk_copy(sbe_ref[0], ci).start()
        for_each_row(units(sbr_ref[0]), lambda i: row_copy(tok0_ref, 0, i).start())

    @pl.when(valid)
    def _():
        e_cur = sbe_ref[s]
        has_next = s + 1 < nsb
        for_each_row(n_units, lambda i: row_copy(tok0_ref, slot, i).wait())

        @pl.when(has_next)
        def _():
            for_each_row(units(sbr_ref[s_next]), lambda i: row_copy(tokn_ref, 1 - slot, i).start())

        for ci in range(FFN_CHUNKS):
            chunk_copy(e_cur, ci).wait()
            wb = wb_ref.at[ci % 2]
            wb[...] = wbuf_ref[ci % FFN_NBUF].astype(BF16)
            ahead = ci + FFN_NBUF
            if ahead < FFN_CHUNKS:
                chunk_copy(e_cur, ahead).start()
            else:
                @pl.when(has_next)
                def _(ahead=ahead):
                    chunk_copy(sbe_ref[s_next], ahead - FFN_CHUNKS).start()

            if ci < 2 * FFN_UP_CHUNKS:
                k, is_up = divmod(ci, 2)

                def block(r0, m, k=k, is_up=is_up, wb=wb):
                    r = slice(r0, r0 + m)
                    if not is_up and k % 2 == 0:
                        first = r0 * ROW_TILES + (k // 2) * word_tiles
                        w = jnp.concatenate(
                            [g_ref[slot, pl.ds(first + c, m, stride=ROW_TILES), :] for c in range(word_tiles)],
                            axis=1)
                        xk_ref[0, r, :] = _unpack_lo(w).astype(BF16)
                        xk_ref[1, r, :] = _unpack_hi(w).astype(BF16)
                    acc_ref = uacc_ref if is_up else gacc_ref
                    d = jnp.dot(xk_ref[k % 2, r, :], wb[...], preferred_element_type=F32)
                    if k > 0:
                        d = acc_ref[r, :] + d
                    if is_up and k == FFN_UP_CHUNKS - 1:
                        g = gacc_ref[r, :]
                        a_ref[r, :] = ((g * jax.nn.sigmoid(g)) * d).astype(BF16)
                    else:
                        acc_ref[r, :] = d
            else:
                n = ci - 2 * FFN_UP_CHUNKS

                def block(r0, m, n=n, wb=wb):
                    y = jnp.dot(a_ref[r0:r0 + m, :], wb[...], preferred_element_type=F32)
                    half = FFN_NC // 2
                    ys_ref[r0:r0 + m, n * half:(n + 1) * half] = _pack_bf16_pair(y[:, :half], y[:, half:])

            for_row_blocks(block)

        for u in range(rows // ROW_UNIT):
            @pl.when(n_units <= u)
            def _(u=u):
                ys_ref[u * ROW_UNIT:(u + 1) * ROW_UNIT, :] = jnp.zeros((ROW_UNIT, ys_ref.shape[1]), U32)

    @pl.when(jnp.logical_not(valid))
    def _():
        ys_ref[...] = jnp.zeros_like(ys_ref)


def _ffn(sb_e, sb_rows, nsb, tok3, hp, w_gate, w_up, w_down, *, rows):
    s_max = sb_e.shape[0]
    assert rows % (2 * ROW_UNIT) == 0
    return pl.pallas_call(
        functools.partial(_ffn_body, rows=rows),
        grid_spec=pltpu.PrefetchScalarGridSpec(
            num_scalar_prefetch=3,
            grid=(s_max,),
            in_specs=[
                pl.BlockSpec((1, 1, rows), lambda s, sbe, sbr, nsb: (0, 0, 0), memory_space=pltpu.SMEM),
                pl.BlockSpec((1, 1, rows), lambda s, sbe, sbr, nsb: (jnp.minimum(s + 1, s_max - 1), 0, 0),
                             memory_space=pltpu.SMEM),
                pl.BlockSpec(memory_space=pl.ANY),
                pl.BlockSpec(memory_space=pl.ANY),
                pl.BlockSpec(memory_space=pl.ANY),
                pl.BlockSpec(memory_space=pl.ANY),
            ],
            out_specs=pl.BlockSpec((rows, HALF), lambda s, sbe, sbr, nsb: (s, 0)),
            scratch_shapes=[
                pltpu.VMEM((2, rows * ROW_TILES, LANES), U32),
                pltpu.SemaphoreType.DMA((2,)),
                pltpu.VMEM((FFN_NBUF, FFN_KC, D_EXPERT), F32),
                pltpu.SemaphoreType.DMA((FFN_NBUF,)),
                pltpu.VMEM((2, FFN_KC, D_EXPERT), BF16),
                pltpu.VMEM((2, rows, FFN_KC), BF16),
                pltpu.VMEM((rows, D_EXPERT), F32),
                pltpu.VMEM((rows, D_EXPERT), F32),
                pltpu.VMEM((rows, D_EXPERT), BF16),
            ],
        ),
        out_shape=jax.ShapeDtypeStruct((s_max * rows, HALF), U32),
        compiler_params=_params(("arbitrary",)),
        name="moe_ffn",
    )(sb_e, sb_rows, nsb, tok3, tok3, hp, w_gate, w_up, w_down)


def _combine_body(pos0_ref, posn_ref, h1_ref, gate_ref, gf_ref, ys_hbm, out_ref, buf_ref, sem, *, tb):
    i = pl.program_id(0)
    slot = lax.rem(i, 2)

    def row_copy(pos_ref, slot_, k, j):
        src = ys_hbm.at[pl.ds(pos_ref[0, k, j], 1)]
        return pltpu.make_async_copy(src, buf_ref.at[slot_, k, pl.ds(j, 1)], sem.at[slot_])

    def for_each_row(fn):
        def body(j, c):
            fn(0, j)
            fn(1, j)
            return c
        lax.fori_loop(0, tb, body, 0, unroll=DMA_UNROLL)

    @pl.when(i == 0)
    def _():
        for_each_row(lambda k, j: row_copy(pos0_ref, 0, k, j).start())

    for_each_row(lambda k, j: row_copy(pos0_ref, slot, k, j).wait())

    @pl.when(i + 1 < pl.num_programs(0))
    def _():
        for_each_row(lambda k, j: row_copy(posn_ref, 1 - slot, k, j).start())

    gate = gate_ref[...]
    half = FFN_NC // 2
    for n in range(FFN_DOWN_CHUNKS):
        w0 = buf_ref[slot, 0, :, n * half:(n + 1) * half]
        w1 = buf_ref[slot, 1, :, n * half:(n + 1) * half]
        for unpack, c0 in ((_unpack_lo, n * FFN_NC), (_unpack_hi, n * FFN_NC + half)):
            cols = slice(c0, c0 + half)
            y = unpack(w0) * gate[:, 0:1] + unpack(w1) * gate[:, 1:2]
            out_ref[:, cols] = h1_ref[:, cols] + y
    h = out_ref[...]
    ms = jnp.mean(h * h, axis=-1, keepdims=True)
    out_ref[...] = (h * lax.rsqrt(ms + NORM_EPS)) * gf_ref[...]


def _combine(pos3, h1, gate, gf, ys, *, tb):
    t = h1.shape[0]
    n_blocks = t // tb
    return pl.pallas_call(
        functools.partial(_combine_body, tb=tb),
        grid=(n_blocks,),
        in_specs=[
            pl.BlockSpec((1, 2, tb), lambda i: (0, 0, 0), memory_space=pltpu.SMEM),
            pl.BlockSpec((1, 2, tb), lambda i: (jnp.minimum(i + 1, n_blocks - 1), 0, 0), memory_space=pltpu.SMEM),
            pl.BlockSpec((tb, D_MODEL), lambda i: (i, 0)),
            pl.BlockSpec((tb, LANES), lambda i: (i, 0)),
            pl.BlockSpec((1, D_MODEL), lambda i: (0, 0)),
            pl.BlockSpec(memory_space=pl.ANY),
        ],
        out_specs=pl.BlockSpec((tb, D_MODEL), lambda i: (i, 0)),
        out_shape=jax.ShapeDtypeStruct((t, D_MODEL), F32),
        scratch_shapes=[pltpu.VMEM((2, 2, tb, HALF), U32), pltpu.SemaphoreType.DMA((2,))],
        compiler_params=_params(("arbitrary",)),
        name="moe_combine",
    )(pos3, pos3, h1, gate, gf, ys)


def _rank_body(eid_ref, rank_ref, counts_ref, carry_ref):
    @pl.when(pl.program_id(0) == 0)
    def _():
        carry_ref[...] = jnp.zeros_like(carry_ref)

    tb = eid_ref.shape[0]
    eid = eid_ref[...]
    lane = lax.broadcasted_iota(I32, eid.shape, 1)
    oh0 = lane == eid[:, 0:1]
    oh1 = lane == eid[:, 1:2]
    hits = jnp.where(oh0 | oh1, 1.0, 0.0)
    row = lax.broadcasted_iota(I32, (tb, tb), 0)
    col = lax.broadcasted_iota(I32, (tb, tb), 1)
    earlier = jnp.where(col < row, 1.0, 0.0).astype(BF16)
    before = jnp.dot(earlier, hits.astype(BF16), preferred_element_type=F32) + carry_ref[...]
    r0 = jnp.sum(jnp.where(oh0, before, 0.0), axis=-1, keepdims=True)
    r1 = jnp.sum(jnp.where(oh1, before, 0.0), axis=-1, keepdims=True)
    rank_ref[...] = jnp.where(lane == 0, r0, jnp.where(lane == 1, r1, 0.0)).astype(I32)
    carry_ref[...] = carry_ref[...] + jnp.sum(hits, axis=0, keepdims=True)
    counts_ref[...] = carry_ref[...].astype(I32)


def _rank(eid, *, tb):
    t = eid.shape[0]
    return pl.pallas_call(
        _rank_body,
        grid=(t // tb,),
        in_specs=[pl.BlockSpec((tb, LANES), lambda i: (i, 0))],
        out_specs=[pl.BlockSpec((tb, LANES), lambda i: (i, 0)), pl.BlockSpec((1, LANES), lambda i: (0, 0))],
        out_shape=[jax.ShapeDtypeStruct((t, LANES), I32), jax.ShapeDtypeStruct((1, LANES), I32)],
        scratch_shapes=[pltpu.VMEM((1, LANES), F32)],
        compiler_params=_params(("arbitrary",)),
        name="moe_rank",
    )(eid)


def _dispatch_plan(eid, *, rows, s_max):
    t = eid.shape[0]
    rank2, counts_row = _rank(eid, tb=512)
    flat_e = eid[:, :2].reshape(-1)
    rank = rank2[:, :2].reshape(-1)
    counts = counts_row[0, :N_EXPERTS]
    nsb_e = (counts + rows - 1) // rows
    sb_end = jnp.cumsum(nsb_e)
    sb_start = sb_end - nsb_e
    nsb = sb_end[-1]
    pos = sb_start[flat_e] * rows + rank
    s_idx = jnp.minimum(jnp.arange(s_max, dtype=I32), nsb - 1)
    sb_e = jnp.minimum(jnp.searchsorted(sb_end, s_idx, side="right"), N_EXPERTS - 1).astype(I32)
    sb_rows = jnp.clip(counts[sb_e] - (s_idx - sb_start[sb_e]) * rows, 0, rows)
    sb_rows = jnp.where(jnp.arange(s_max) < nsb, sb_rows, 0).astype(I32)
    flat_tok = jnp.arange(2 * t, dtype=I32) // 2
    tok = jnp.zeros((s_max * rows,), I32).at[pos].set(flat_tok)
    return sb_e, sb_rows, nsb.reshape(1).astype(I32), tok.reshape(s_max, 1, rows), pos.reshape(t, 2)


def _layer(x, norm1_g, w_in, gmlp_v_g, gmlp_ws, gmlp_bs, gla_wa2, gla_ba, gla_norm_g, w_out, norm2_g,
           router_coarse_w, router_coarse_b, router_fine_w, router_fine_b, exp_w_gate, exp_w_up, exp_w_down,
           norm_f_g, *, tm_in, tb_gmlp, tb_gla, tm_out, tb_comb, sb_rows):
    t = x.shape[0]
    w_main = _to_bf16(w_in, tr=256)
    w_lr = jnp.pad(w_in[:, D_PROJ_MAIN:], ((0, 0), (0, LANES - GLA_GATE_RANK))).astype(BF16)
    wa2p = jnp.pad(gla_wa2, ((0, LANES - GLA_GATE_RANK), (0, 0))).astype(BF16)
    wo = _to_bf16(w_out, tr=512)
    n_route = N_GROUPS + N_EXPERTS
    wr = jnp.pad(jnp.concatenate([router_coarse_w, router_fine_w], axis=1),
                 ((0, 0), (0, LANES - n_route))).astype(BF16)
    br = jnp.pad(jnp.concatenate([router_coarse_b, router_fine_b]), (0, LANES - n_route)).reshape(1, LANES)

    z, lr = _inproj(x, norm1_g.reshape(1, -1), w_main, w_lr, tm=tm_in, tn=1024)
    y_a = _gmlp(z, gmlp_v_g.reshape(1, -1), gmlp_ws, gmlp_bs.T, tb=tb_gmlp)
    y_b = _gla(z, lr, wa2p, gla_ba.reshape(1, -1), gla_norm_g.reshape(1, -1), tb=tb_gla)
    h1, hp, gate, eid = _outproj(y_a, y_b, x, wo, norm2_g.reshape(1, -1), wr, br, tm=tm_out, tn=512)

    s_max = N_EXPERTS + (2 * t) // sb_rows
    sb_e, sb_n, nsb, tok3, pos = _dispatch_plan(eid, rows=sb_rows, s_max=s_max)
    ys = _ffn(sb_e, sb_n, nsb, tok3, hp, exp_w_gate, exp_w_up, exp_w_down, rows=sb_rows)
    pos3 = pos.reshape(t // tb_comb, tb_comb, 2).transpose(0, 2, 1)
    return _combine(pos3, h1, gate, norm_f_g.reshape(1, -1), ys, tb=tb_comb)


def kernel(x, norm1_g, w_in, gmlp_v_g, gmlp_ws, gmlp_bs, gla_wa2, gla_ba, gla_norm_g, w_out, norm2_g,
           router_coarse_w, router_coarse_b, router_fine_w, router_fine_b, exp_w_gate, exp_w_up, exp_w_down,
           norm_f_g):
    b, t, d = x.shape
    assert b == 1 and d == D_MODEL and norm1_g.shape[0] == 1, "one sequence, one layer, the stated widths"

    def first(a):
        return a.reshape(a.shape[1:])

    out = _layer(
        first(x), first(norm1_g), first(w_in), first(gmlp_v_g), first(gmlp_ws), first(gmlp_bs), first(gla_wa2),
        first(gla_ba), first(gla_norm_g), first(w_out), first(norm2_g), first(router_coarse_w),
        first(router_coarse_b), first(router_fine_w), first(router_fine_b), first(exp_w_gate), first(exp_w_up),
        first(exp_w_down), norm_f_g,
        tm_in=512, tb_gmlp=256, tb_gla=512, tm_out=512, tb_comb=512, sb_rows=768)
    return out.reshape(b, t, d)
```

```python
import functools

import jax
import jax.numpy as jnp
from jax import lax
from jax.experimental import pallas as pl
from jax.experimental.pallas import tpu as pltpu

F32 = jnp.float32
BF16 = jnp.bfloat16
U32 = jnp.uint32
I32 = jnp.int32

D_MODEL = 4096
GMLP_WIDTH = 2048
GMLP_HEADS = 8
GMLP_HEAD_DIM = GMLP_WIDTH // GMLP_HEADS
GMLP_CHUNK = 128
GLA_WIDTH = 2048
GLA_HEADS = 4
GLA_KEY_WIDTH = 1024
GLA_HEAD_DK = GLA_KEY_WIDTH // GLA_HEADS
GLA_HEAD_DV = GLA_WIDTH // GLA_HEADS
GLA_GATE_RANK = 16
GLA_GATE_NORMALIZER = 16.0
GLA_CHUNK = 64
N_GROUPS = 8
EXPERTS_PER_GROUP = 8
N_EXPERTS = N_GROUPS * EXPERTS_PER_GROUP
D_EXPERT = D_MODEL // 4
NORM_EPS = 1e-6
D_PROJ_MAIN = 2 * GMLP_WIDTH + 2 * GLA_KEY_WIDTH + 2 * GLA_WIDTH

LANES = 128
HALF = D_MODEL // 2
ROW_TILES = HALF // LANES
FFN_KC = 1024
VMEM_LIMIT = 58 * 1024 * 1024

HIGHEST = lax.Precision.HIGHEST


def _params(semantics, vmem=VMEM_LIMIT):
    return pltpu.CompilerParams(dimension_semantics=semantics, vmem_limit_bytes=vmem)


def _inproj_body(x_ref, g_ref, w_ref, wlr_ref, z_ref, lr_ref, xn_ref):
    @pl.when(pl.program_id(1) == 0)
    def _():
        x = x_ref[...]
        ms = jnp.mean(x * x, axis=-1, keepdims=True)
        xn = ((x * lax.rsqrt(ms + NORM_EPS)) * g_ref[...]).astype(BF16)
        xn_ref[...] = xn
        lr_ref[...] = jnp.dot(xn, wlr_ref[...], preferred_element_type=F32)

    z_ref[...] = jnp.dot(xn_ref[...], w_ref[...], preferred_element_type=F32)


def _inproj(x, g, w_main, w_lr, *, tm, tn):
    t = x.shape[0]
    return pl.pallas_call(
        _inproj_body,
        grid=(t // tm, D_PROJ_MAIN // tn),
        in_specs=[
            pl.BlockSpec((tm, D_MODEL), lambda i, j: (i, 0)),
            pl.BlockSpec((1, D_MODEL), lambda i, j: (0, 0)),
            pl.BlockSpec((D_MODEL, tn), lambda i, j: (0, j)),
            pl.BlockSpec((D_MODEL, LANES), lambda i, j: (0, 0)),
        ],
        out_specs=[
            pl.BlockSpec((tm, tn), lambda i, j: (i, j)),
            pl.BlockSpec((tm, LANES), lambda i, j: (i, 0)),
        ],
        out_shape=[
            jax.ShapeDtypeStruct((t, D_PROJ_MAIN), F32),
            jax.ShapeDtypeStruct((t, LANES), F32),
        ],
        scratch_shapes=[pltpu.VMEM((tm, D_MODEL), BF16)],
        compiler_params=_params(("arbitrary", "arbitrary")),
        name="inproj",
    )(x, g, w_main, w_lr)


def _gelu(x):
    return x * (lax.erf(x * (2.0 ** -0.5)) + 1.0) * 0.5


def _gmlp_body(u_ref, v_ref, vg_ref, ws_ref, bst_ref, y_ref, *, n_chunks):
    row = lax.broadcasted_iota(I32, (GMLP_CHUNK, GMLP_CHUNK), 0)
    col = lax.broadcasted_iota(I32, (GMLP_CHUNK, GMLP_CHUNK), 1)
    causal = row >= col
    for c in range(n_chunks):
        rows = slice(c * GMLP_CHUNK, (c + 1) * GMLP_CHUNK)
        gv = _gelu(v_ref[rows, :])
        mu = jnp.mean(gv, axis=-1, keepdims=True)
        vc = gv - mu
        var = jnp.mean(vc * vc, axis=-1, keepdims=True)
        vn = ((vc * lax.rsqrt(var + NORM_EPS)) * vg_ref[...]).astype(BF16)
        for h in range(GMLP_HEADS):
            cols = slice(h * GMLP_HEAD_DIM, (h + 1) * GMLP_HEAD_DIM)
            wm = jnp.where(causal, ws_ref[h], 0.0).astype(BF16)
            s = jnp.dot(wm, vn[:, cols], preferred_element_type=F32) + bst_ref[:, h:h + 1]
            y_ref[rows, cols] = (_gelu(u_ref[rows, cols]) * s).astype(BF16)


def _gmlp(z, v_g, ws, bs_t, *, tb):
    t = z.shape[0]
    blk = GMLP_WIDTH
    return pl.pallas_call(
        functools.partial(_gmlp_body, n_chunks=tb // GMLP_CHUNK),
        grid=(t // tb,),
        in_specs=[
            pl.BlockSpec((tb, blk), lambda i: (i, 0)),
            pl.BlockSpec((tb, blk), lambda i: (i, 1)),
            pl.BlockSpec((1, blk), lambda i: (0, 0)),
            pl.BlockSpec((GMLP_HEADS, GMLP_CHUNK, GMLP_CHUNK), lambda i: (0, 0, 0)),
            pl.BlockSpec((GMLP_CHUNK, GMLP_HEADS), lambda i: (0, 0)),
        ],
        out_specs=pl.BlockSpec((tb, blk), lambda i: (i, 0)),
        out_shape=jax.ShapeDtypeStruct((t, blk), BF16),
        compiler_params=_params(("arbitrary",)),
        name="gmlp",
    )(z, z, v_g, ws, bs_t)


def _gla_body(q_ref, k_ref, v_ref, r_ref, lr_ref, wa2_ref, ba_ref, gn_ref, y_ref, st_ref, bcum_ref, *, n_chunks):
    @pl.when(pl.program_id(0) == 0)
    def _():
        st_ref[...] = jnp.zeros_like(st_ref)

    c_len = GLA_CHUNK
    tb = lr_ref.shape[0]
    tn_dims = (((0,), (0,)), ((), ()))
    nt_dims = (((1,), (1,)), ((), ()))

    logit = jnp.dot(lr_ref[...].astype(BF16), wa2_ref[...], preferred_element_type=F32) + ba_ref[...]
    bcum = (jnp.minimum(logit, 0.0) - jnp.log1p(jnp.exp(-jnp.abs(logit)))) * (1.0 / GLA_GATE_NORMALIZER)
    row_in_chunk = lax.broadcasted_iota(I32, (tb, 1), 0) & (c_len - 1)
    step = 1
    while step < c_len:
        bcum = bcum + jnp.where(row_in_chunk >= step, pltpu.roll(bcum, step, axis=0), 0.0)
        step *= 2
    bcum_ref[...] = bcum

    row = lax.broadcasted_iota(I32, (c_len, c_len), 0)
    col = lax.broadcasted_iota(I32, (c_len, c_len), 1)
    causal = row >= col

    def chunk(c, carry):
        t0 = pl.multiple_of(c * c_len, c_len)
        rows = pl.ds(t0, c_len)
        for h in range(GLA_HEADS):
            kc = slice(h * GLA_HEAD_DK, (h + 1) * GLA_HEAD_DK)
            vc = slice(h * GLA_HEAD_DV, (h + 1) * GLA_HEAD_DV)
            bcum = bcum_ref[rows, kc]
            b_last = bcum[c_len - 1:c_len, :]
            q = q_ref[rows, kc] * (GLA_HEAD_DK ** -0.5)
            k = k_ref[rows, kc]
            q_e = (q * jnp.exp(bcum)).astype(BF16)
            k_e = (k * jnp.exp(-bcum)).astype(BF16)
            k_end = (k * jnp.exp(b_last - bcum)).astype(BF16)
            v = v_ref[rows, vc].astype(BF16)
            scores = lax.dot_general(q_e, k_e, nt_dims, preferred_element_type=F32)
            scores = jnp.where(causal, scores, 0.0).astype(BF16)
            state_t = st_ref[h]
            o = jnp.dot(scores, v, preferred_element_type=F32)
            o = o + lax.dot_general(q_e, state_t.astype(BF16), nt_dims, preferred_element_type=F32)
            kv_t = lax.dot_general(v, k_end, tn_dims, preferred_element_type=F32)
            st_ref[h] = state_t * jnp.exp(b_last) + kv_t
            o = (o * lax.rsqrt(jnp.mean(o * o, axis=-1, keepdims=True) + NORM_EPS)) * gn_ref[...]
            r = r_ref[rows, vc]
            y_ref[rows, vc] = (o * (r * jax.nn.sigmoid(r))).astype(BF16)
        return carry

    lax.fori_loop(0, n_chunks, chunk, 0)


def _gla(z, lr, wa2p, ba, gn, *, tb):
    t = z.shape[0]
    kw, vw = GLA_KEY_WIDTH, GLA_WIDTH
    return pl.pallas_call(
        functools.partial(_gla_body, n_chunks=tb // GLA_CHUNK),
        grid=(t // tb,),
        in_specs=[
            pl.BlockSpec((tb, kw), lambda i: (i, 4)),
            pl.BlockSpec((tb, kw), lambda i: (i, 5)),
            pl.BlockSpec((tb, vw), lambda i: (i, 3)),
            pl.BlockSpec((tb, vw), lambda i: (i, 4)),
            pl.BlockSpec((tb, LANES), lambda i: (i, 0)),
            pl.BlockSpec((LANES, kw), lambda i: (0, 0)),
            pl.BlockSpec((1, kw), lambda i: (0, 0)),
            pl.BlockSpec((1, GLA_HEAD_DV), lambda i: (0, 0)),
        ],
        out_specs=pl.BlockSpec((tb, vw), lambda i: (i, 0)),
        out_shape=jax.ShapeDtypeStruct((t, vw), BF16),
        scratch_shapes=[pltpu.VMEM((GLA_HEADS, GLA_HEAD_DV, GLA_HEAD_DK), F32),
                        pltpu.VMEM((tb, kw), F32)],
        compiler_params=_params(("arbitrary",)),
        name="gla",
    )(z, z, z, z, lr, wa2p, ba, gn)


def _bf16_bits(x):
    return lax.bitcast_convert_type(x.astype(BF16).astype(F32), U32)


def _outproj_body(ya_ref, yb_ref, x_ref, woa_ref, wob_ref, g2_ref, wr_ref, br_ref,
                  h1_ref, hp_ref, gate_ref, eid_ref, hacc_ref, *, nj, tn):
    j = pl.program_id(1)
    tm = x_ref.shape[0]
    acc = jnp.dot(ya_ref[...], woa_ref[...], preferred_element_type=F32)
    acc = acc + jnp.dot(yb_ref[...], wob_ref[...], preferred_element_type=F32)
    h = x_ref[...] + acc
    h1_ref[...] = h
    hacc_ref[j] = h

    @pl.when(j == nj - 1)
    def _():
        ssq = jnp.sum(hacc_ref[0] * hacc_ref[0], axis=-1, keepdims=True)
        for jj in range(1, nj):
            ssq = ssq + jnp.sum(hacc_ref[jj] * hacc_ref[jj], axis=-1, keepdims=True)
        rstd = lax.rsqrt(ssq * (1.0 / D_MODEL) + NORM_EPS)
        logits = br_ref[...]

        def normed(jj):
            cols = slice(jj * tn, (jj + 1) * tn)
            hn = (hacc_ref[jj] * rstd) * g2_ref[:, cols]
            return hn, jnp.dot(hn.astype(BF16), wr_ref[cols, :], preferred_element_type=F32)

        kb = FFN_KC // tn
        for wb in range(nj // 2):
            q, hh = divmod(wb, kb)
            lo, d_lo = normed(2 * q * kb + hh)
            hi, d_hi = normed((2 * q + 1) * kb + hh)
            logits = logits + d_lo + d_hi
            packed = (_bf16_bits(hi) & jnp.uint32(0xFFFF0000)) | (_bf16_bits(lo) >> 16)
            for c in range(tn // LANES):
                ct = wb * (tn // LANES) + c
                hp_ref[pl.ds(ct, tm, stride=ROW_TILES), :] = packed[:, c * LANES:(c + 1) * LANES]

        lane = lax.broadcasted_iota(I32, logits.shape, 1)
        neg = jnp.float32(-jnp.inf)
        big = jnp.int32(LANES)
        cmask = lane < N_GROUPS
        cl = jnp.where(cmask, logits, neg)
        cmax = jnp.max(cl, axis=-1, keepdims=True)
        g_sel = jnp.min(jnp.where(cl == cmax, lane, big), axis=-1, keepdims=True)
        p_sel = 1.0 / jnp.sum(jnp.where(cmask, jnp.exp(logits - cmax), 0.0), axis=-1, keepdims=True)
        f0 = N_GROUPS + g_sel * EXPERTS_PER_GROUP
        fl = jnp.where((lane >= f0) & (lane < f0 + EXPERTS_PER_GROUP), logits, neg)
        v1 = jnp.max(fl, axis=-1, keepdims=True)
        i1 = jnp.min(jnp.where(fl == v1, lane, big), axis=-1, keepdims=True)
        fl2 = jnp.where(lane == i1, neg, fl)
        v2 = jnp.max(fl2, axis=-1, keepdims=True)
        i2 = jnp.min(jnp.where(fl2 == v2, lane, big), axis=-1, keepdims=True)
        e2 = jnp.exp(v2 - v1)
        w1 = p_sel / (1.0 + e2)
        w2 = p_sel * e2 / (1.0 + e2)
        gate_ref[...] = jnp.where(lane == 0, w1, jnp.where(lane == 1, w2, 0.0))
        eid_ref[...] = jnp.where(lane == 0, i1 - N_GROUPS, jnp.where(lane == 1, i2 - N_GROUPS, 0))


def _outproj(ya, yb, x, wo, g2, wr, br, *, tm, tn):
    t = x.shape[0]
    nj = D_MODEL // tn
    assert FFN_KC % tn == 0 and GMLP_WIDTH == GLA_WIDTH
    return pl.pallas_call(
        functools.partial(_outproj_body, nj=nj, tn=tn),
        grid=(t // tm, nj),
        in_specs=[
            pl.BlockSpec((tm, GMLP_WIDTH), lambda i, j: (i, 0)),
            pl.BlockSpec((tm, GLA_WIDTH), lambda i, j: (i, 0)),
            pl.BlockSpec((tm, tn), lambda i, j: (i, j)),
            pl.BlockSpec((GMLP_WIDTH, tn), lambda i, j: (0, j)),
            pl.BlockSpec((GLA_WIDTH, tn), lambda i, j: (1, j)),
            pl.BlockSpec((1, D_MODEL), lambda i, j: (0, 0)),
            pl.BlockSpec((D_MODEL, LANES), lambda i, j: (0, 0)),
            pl.BlockSpec((1, LANES), lambda i, j: (0, 0)),
        ],
        out_specs=[
            pl.BlockSpec((tm, tn), lambda i, j: (i, j)),
            pl.BlockSpec((tm * ROW_TILES, LANES), lambda i, j: (i, 0)),
            pl.BlockSpec((tm, LANES), lambda i, j: (i, 0)),
            pl.BlockSpec((tm, LANES), lambda i, j: (i, 0)),
        ],
        out_shape=[
            jax.ShapeDtypeStruct((t, D_MODEL), F32),
            jax.ShapeDtypeStruct((t * ROW_TILES, LANES), U32),
            jax.ShapeDtypeStruct((t, LANES), F32),
            jax.ShapeDtypeStruct((t, LANES), I32),
        ],
        scratch_shapes=[pltpu.VMEM((nj, tm, tn), F32)],
        compiler_params=_params(("arbitrary", "arbitrary")),
        name="outproj_router",
    )(ya, yb, x, wo, wo, g2, wr, br)


FFN_UP_CHUNKS = D_MODEL // FFN_KC
FFN_NC = 1024
FFN_DOWN_CHUNKS = D_MODEL // FFN_NC
FFN_CHUNKS = 2 * FFN_UP_CHUNKS + FFN_DOWN_CHUNKS
FFN_NBUF = 4
ROW_UNIT = 128
DMA_UNROLL = 8
WEIGHT_DMA_PRIORITY = 1
assert FFN_CHUNKS % FFN_NBUF == 0 and FFN_KC == D_EXPERT == FFN_NC


def _pack_bf16_pair(lo, hi):
    return (_bf16_bits(hi) & jnp.uint32(0xFFFF0000)) | (_bf16_bits(lo) >> 16)


def _unpack_lo(w):
    return lax.bitcast_convert_type(w << 16, F32)


def _unpack_hi(w):
    return lax.bitcast_convert_type(w & jnp.uint32(0xFFFF0000), F32)


def _ffn_body(sbe_ref, sbr_ref, nsb_ref, tok0_ref, tokn_ref, hp_hbm, wg_hbm, wu_hbm, wd_hbm, ys_ref,
              g_ref, gsem, wbuf_ref, wsem, wb_ref, xk_ref, gacc_ref, uacc_ref, a_ref, *, rows):
    s = pl.program_id(0)
    nsb = nsb_ref[0]
    valid = s < nsb
    slot = lax.rem(s, 2)
    s_next = jnp.minimum(s + 1, sbr_ref.shape[0] - 1)
    word_tiles = FFN_KC // LANES

    def units(n_rows):
        return lax.shift_right_logical(n_rows + (ROW_UNIT - 1), ROW_UNIT.bit_length() - 1)

    n_units = units(sbr_ref[s])

    def row_copy(tok_ref, slot_, i):
        src = hp_hbm.at[pl.ds(pl.multiple_of(tok_ref[0, 0, i] * ROW_TILES, ROW_TILES), ROW_TILES)]
        dst = g_ref.at[slot_, pl.ds(pl.multiple_of(i * ROW_TILES, ROW_TILES), ROW_TILES)]
        return pltpu.make_async_copy(src, dst, gsem.at[slot_])

    def for_each_row(n_units_, fn):
        def unit_body(u, c):
            def body(i, c2):
                fn(u * ROW_UNIT + i)
                return c2
            return lax.fori_loop(0, ROW_UNIT, body, c, unroll=DMA_UNROLL)
        lax.fori_loop(0, n_units_, unit_body, 0)

    def for_row_blocks(fn):
        pair = 2 * ROW_UNIT
        for j in range(rows // pair):
            @pl.when(n_units >= 2 * j + 2)
            def _(j=j):
                fn(pair * j, pair)

            @pl.when(n_units == 2 * j + 1)
            def _(j=j):
                fn(pair * j, ROW_UNIT)

    def chunk_copy(e, ci):
        if ci < 2 * FFN_UP_CHUNKS:
            w_hbm = wg_hbm if ci % 2 == 0 else wu_hbm
            src = w_hbm.at[e, pl.ds((ci // 2) * FFN_KC, FFN_KC), :]
        else:
            src = wd_hbm.at[e, :, pl.ds((ci - 2 * FFN_UP_CHUNKS) * FFN_NC, FFN_NC)]
        b = ci % FFN_NBUF
        return pltpu.make_async_copy(src, wbuf_ref.at[b], wsem.at[b])

    @pl.when(s == 0)
    def _():
        for ci in range(FFN_NBUF):
            chunk_copy(sbe_ref[0], ci).start(priority=WEIGHT_DMA_PRIORITY)
        for_each_row(units(sbr_ref[0]), lambda i: row_copy(tok0_ref, 0, i).start())

    @pl.when(valid)
    def _():
        e_cur = sbe_ref[s]
        has_next = s + 1 < nsb
        for_each_row(n_units, lambda i: row_copy(tok0_ref, slot, i).wait())

        @pl.when(has_next)
        def _():
            for_each_row(units(sbr_ref[s_next]), lambda i: row_copy(tokn_ref, 1 - slot, i).start())

        for ci in range(FFN_CHUNKS):
            chunk_copy(e_cur, ci).wait()
            wb = wb_ref.at[ci % 2]
            wb[...] = wbuf_ref[ci % FFN_NBUF].astype(BF16)
            ahead = ci + FFN_NBUF
            if ahead < FFN_CHUNKS:
                chunk_copy(e_cur, ahead).start(priority=WEIGHT_DMA_PRIORITY)
            else:
                @pl.when(has_next)
                def _(ahead=ahead):
                    chunk_copy(sbe_ref[s_next], ahead - FFN_CHUNKS).start(priority=WEIGHT_DMA_PRIORITY)

            if ci < 2 * FFN_UP_CHUNKS:
                k, is_up = divmod(ci, 2)

                def block(r0, m, k=k, is_up=is_up, wb=wb):
                    r = slice(r0, r0 + m)
                    if not is_up and k % 2 == 0:
                        first = r0 * ROW_TILES + (k // 2) * word_tiles
                        w = jnp.concatenate(
                            [g_ref[slot, pl.ds(first + c, m, stride=ROW_TILES), :] for c in range(word_tiles)],
                            axis=1)
                        xk_ref[0, r, :] = _unpack_lo(w).astype(BF16)
                        xk_ref[1, r, :] = _unpack_hi(w).astype(BF16)
                    acc_ref = uacc_ref if is_up else gacc_ref
                    d = jnp.dot(xk_ref[k % 2, r, :], wb[...], preferred_element_type=F32)
                    if k > 0:
                        d = acc_ref[r, :] + d
                    if is_up and k == FFN_UP_CHUNKS - 1:
                        g = gacc_ref[r, :]
                        a_ref[r, :] = ((g * jax.nn.sigmoid(g)) * d).astype(BF16)
                    else:
                        acc_ref[r, :] = d
            else:
                n = ci - 2 * FFN_UP_CHUNKS

                def block(r0, m, n=n, wb=wb):
                    y = jnp.dot(a_ref[r0:r0 + m, :], wb[...], preferred_element_type=F32)
                    half = FFN_NC // 2
                    ys_ref[r0:r0 + m, n * half:(n + 1) * half] = _pack_bf16_pair(y[:, :half], y[:, half:])

            for_row_blocks(block)

        for u in range(rows // ROW_UNIT):
            @pl.when(n_units <= u)
            def _(u=u):
                ys_ref[u * ROW_UNIT:(u + 1) * ROW_UNIT, :] = jnp.zeros((ROW_UNIT, ys_ref.shape[1]), U32)

    @pl.when(jnp.logical_not(valid))
    def _():
        ys_ref[...] = jnp.zeros_like(ys_ref)


def _ffn(sb_e, sb_rows, nsb, tok3, hp, w_gate, w_up, w_down, *, rows):
    s_max = sb_e.shape[0]
    assert rows % (2 * ROW_UNIT) == 0
    return pl.pallas_call(
        functools.partial(_ffn_body, rows=rows),
        grid_spec=pltpu.PrefetchScalarGridSpec(
            num_scalar_prefetch=3,
            grid=(s_max,),
            in_specs=[
                pl.BlockSpec((1, 1, rows), lambda s, sbe, sbr, nsb: (0, 0, 0), memory_space=pltpu.SMEM),
                pl.BlockSpec((1, 1, rows), lambda s, sbe, sbr, nsb: (jnp.minimum(s + 1, s_max - 1), 0, 0),
                             memory_space=pltpu.SMEM),
                pl.BlockSpec(memory_space=pl.ANY),
                pl.BlockSpec(memory_space=pl.ANY),
                pl.BlockSpec(memory_space=pl.ANY),
                pl.BlockSpec(memory_space=pl.ANY),
            ],
            out_specs=pl.BlockSpec((rows, HALF), lambda s, sbe, sbr, nsb: (s, 0)),
            scratch_shapes=[
                pltpu.VMEM((2, rows * ROW_TILES, LANES), U32),
                pltpu.SemaphoreType.DMA((2,)),
                pltpu.VMEM((FFN_NBUF, FFN_KC, D_EXPERT), F32),
                pltpu.SemaphoreType.DMA((FFN_NBUF,)),
                pltpu.VMEM((2, FFN_KC, D_EXPERT), BF16),
                pltpu.VMEM((2, rows, FFN_KC), BF16),
                pltpu.VMEM((rows, D_EXPERT), F32),
                pltpu.VMEM((rows, D_EXPERT), F32),
                pltpu.VMEM((rows, D_EXPERT), BF16),
            ],
        ),
        out_shape=jax.ShapeDtypeStruct((s_max * rows, HALF), U32),
        compiler_params=_params(("arbitrary",)),
        name="moe_ffn",
    )(sb_e, sb_rows, nsb, tok3, tok3, hp, w_gate, w_up, w_down)


def _combine_body(pos0_ref, posn_ref, h1_ref, gate_ref, gf_ref, ys_hbm, out_ref, buf_ref, sem, *, tb):
    i = pl.program_id(0)
    slot = lax.rem(i, 2)

    def row_copy(pos_ref, slot_, k, j):
        src = ys_hbm.at[pl.ds(pos_ref[0, k, j], 1)]
        return pltpu.make_async_copy(src, buf_ref.at[slot_, k, pl.ds(j, 1)], sem.at[slot_])

    def for_each_row(fn):
        def body(j, c):
            fn(0, j)
            fn(1, j)
            return c
        lax.fori_loop(0, tb, body, 0, unroll=DMA_UNROLL)

    @pl.when(i == 0)
    def _():
        for_each_row(lambda k, j: row_copy(pos0_ref, 0, k, j).start(priority=k))

    for_each_row(lambda k, j: row_copy(pos0_ref, slot, k, j).wait())

    @pl.when(i + 1 < pl.num_programs(0))
    def _():
        for_each_row(lambda k, j: row_copy(posn_ref, 1 - slot, k, j).start(priority=k))

    gate = gate_ref[...]
    half = FFN_NC // 2
    for n in range(FFN_DOWN_CHUNKS):
        w0 = buf_ref[slot, 0, :, n * half:(n + 1) * half]
        w1 = buf_ref[slot, 1, :, n * half:(n + 1) * half]
        for unpack, c0 in ((_unpack_lo, n * FFN_NC), (_unpack_hi, n * FFN_NC + half)):
            cols = slice(c0, c0 + half)
            y = unpack(w0) * gate[:, 0:1] + unpack(w1) * gate[:, 1:2]
            out_ref[:, cols] = h1_ref[:, cols] + y
    h = out_ref[...]
    ms = jnp.mean(h * h, axis=-1, keepdims=True)
    out_ref[...] = (h * lax.rsqrt(ms + NORM_EPS)) * gf_ref[...]


def _combine(pos3, h1, gate, gf, ys, *, tb):
    t = h1.shape[0]
    n_blocks = t // tb
    return pl.pallas_call(
        functools.partial(_combine_body, tb=tb),
        grid=(n_blocks,),
        in_specs=[
            pl.BlockSpec((1, 2, tb), lambda i: (0, 0, 0), memory_space=pltpu.SMEM),
            pl.BlockSpec((1, 2, tb), lambda i: (jnp.minimum(i + 1, n_blocks - 1), 0, 0), memory_space=pltpu.SMEM),
            pl.BlockSpec((tb, D_MODEL), lambda i: (i, 0)),
            pl.BlockSpec((tb, LANES), lambda i: (i, 0)),
            pl.BlockSpec((1, D_MODEL), lambda i: (0, 0)),
            pl.BlockSpec(memory_space=pl.ANY),
        ],
        out_specs=pl.BlockSpec((tb, D_MODEL), lambda i: (i, 0)),
        out_shape=jax.ShapeDtypeStruct((t, D_MODEL), F32),
        scratch_shapes=[pltpu.VMEM((2, 2, tb, HALF), U32), pltpu.SemaphoreType.DMA((2,))],
        compiler_params=_params(("arbitrary",)),
        name="moe_combine",
    )(pos3, pos3, h1, gate, gf, ys)


def _rank_body(eid_ref, rank_ref, counts_ref, carry_ref):
    @pl.when(pl.program_id(0) == 0)
    def _():
        carry_ref[...] = jnp.zeros_like(carry_ref)

    tb = eid_ref.shape[0]
    eid = eid_ref[...]
    lane = lax.broadcasted_iota(I32, eid.shape, 1)
    oh0 = lane == eid[:, 0:1]
    oh1 = lane == eid[:, 1:2]
    hits = jnp.where(oh0 | oh1, 1.0, 0.0)
    row = lax.broadcasted_iota(I32, (tb, tb), 0)
    col = lax.broadcasted_iota(I32, (tb, tb), 1)
    earlier = jnp.where(col < row, 1.0, 0.0).astype(BF16)
    before = jnp.dot(earlier, hits.astype(BF16), preferred_element_type=F32) + carry_ref[...]
    r0 = jnp.sum(jnp.where(oh0, before, 0.0), axis=-1, keepdims=True)
    r1 = jnp.sum(jnp.where(oh1, before, 0.0), axis=-1, keepdims=True)
    rank_ref[...] = jnp.where(lane == 0, r0, jnp.where(lane == 1, r1, 0.0)).astype(I32)
    carry_ref[...] = carry_ref[...] + jnp.sum(hits, axis=0, keepdims=True)
    counts_ref[...] = carry_ref[...].astype(I32)


def _rank(eid, *, tb):
    t = eid.shape[0]
    return pl.pallas_call(
        _rank_body,
        grid=(t // tb,),
        in_specs=[pl.BlockSpec((tb, LANES), lambda i: (i, 0))],
        out_specs=[pl.BlockSpec((tb, LANES), lambda i: (i, 0)), pl.BlockSpec((1, LANES), lambda i: (0, 0))],
        out_shape=[jax.ShapeDtypeStruct((t, LANES), I32), jax.ShapeDtypeStruct((1, LANES), I32)],
        scratch_shapes=[pltpu.VMEM((1, LANES), F32)],
        compiler_params=_params(("arbitrary",)),
        name="moe_rank",
    )(eid)


def _dispatch_plan(eid, *, rows, s_max):
    t = eid.shape[0]
    rank2, counts_row = _rank(eid, tb=512)
    flat_e = eid[:, :2].reshape(-1)
    rank = rank2[:, :2].reshape(-1)
    counts = counts_row[0, :N_EXPERTS]
    nsb_e = (counts + rows - 1) // rows
    sb_end = jnp.cumsum(nsb_e)
    sb_start = sb_end - nsb_e
    nsb = sb_end[-1]
    pos = sb_start[flat_e] * rows + rank
    s_idx = jnp.minimum(jnp.arange(s_max, dtype=I32), nsb - 1)
    sb_e = jnp.minimum(jnp.searchsorted(sb_end, s_idx, side="right"), N_EXPERTS - 1).astype(I32)
    sb_rows = jnp.clip(counts[sb_e] - (s_idx - sb_start[sb_e]) * rows, 0, rows)
    sb_rows = jnp.where(jnp.arange(s_max) < nsb, sb_rows, 0).astype(I32)
    flat_tok = jnp.arange(2 * t, dtype=I32) // 2
    tok = jnp.zeros((s_max * rows,), I32).at[pos].set(flat_tok)
    return sb_e, sb_rows, nsb.reshape(1).astype(I32), tok.reshape(s_max, 1, rows), pos.reshape(t, 2)


def _layer(x, norm1_g, w_in, gmlp_v_g, gmlp_ws, gmlp_bs, gla_wa2, gla_ba, gla_norm_g, w_out, norm2_g,
           router_coarse_w, router_coarse_b, router_fine_w, router_fine_b, exp_w_gate, exp_w_up, exp_w_down,
           norm_f_g, *, tm_in, tb_gmlp, tb_gla, tm_out, tb_comb, sb_rows):
    t = x.shape[0]
    w_main = w_in.astype(BF16)
    w_lr = jnp.pad(w_in[:, D_PROJ_MAIN:], ((0, 0), (0, LANES - GLA_GATE_RANK))).astype(BF16)
    wa2p = jnp.pad(gla_wa2, ((0, LANES - GLA_GATE_RANK), (0, 0))).astype(BF16)
    wo = w_out.astype(BF16)
    n_route = N_GROUPS + N_EXPERTS
    wr = jnp.pad(jnp.concatenate([router_coarse_w, router_fine_w], axis=1),
                 ((0, 0), (0, LANES - n_route))).astype(BF16)
    br = jnp.pad(jnp.concatenate([router_coarse_b, router_fine_b]), (0, LANES - n_route)).reshape(1, LANES)

    z, lr = _inproj(x, norm1_g.reshape(1, -1), w_main, w_lr, tm=tm_in, tn=1024)
    y_a = _gmlp(z, gmlp_v_g.reshape(1, -1), gmlp_ws, gmlp_bs.T, tb=tb_gmlp)
    y_b = _gla(z, lr, wa2p, gla_ba.reshape(1, -1), gla_norm_g.reshape(1, -1), tb=tb_gla)
    h1, hp, gate, eid = _outproj(y_a, y_b, x, wo, norm2_g.reshape(1, -1), wr, br, tm=tm_out, tn=512)

    s_max = N_EXPERTS + (2 * t) // sb_rows
    sb_e, sb_n, nsb, tok3, pos = _dispatch_plan(eid, rows=sb_rows, s_max=s_max)
    ys = _ffn(sb_e, sb_n, nsb, tok3, hp, exp_w_gate, exp_w_up, exp_w_down, rows=sb_rows)
    pos3 = pos.reshape(t // tb_comb, tb_comb, 2).transpose(0, 2, 1)
    return _combine(pos3, h1, gate, norm_f_g.reshape(1, -1), ys, tb=tb_comb)


def kernel(x, norm1_g, w_in, gmlp_v_g, gmlp_ws, gmlp_bs, gla_wa2, gla_ba, gla_norm_g, w_out, norm2_g,
           router_coarse_w, router_coarse_b, router_fine_w, router_fine_b, exp_w_gate, exp_w_up, exp_w_down,
           norm_f_g):
    b, t, d = x.shape
    assert b == 1 and d == D_MODEL and norm1_g.shape[0] == 1, "one sequence, one layer, the stated widths"

    def first(a):
        return a.reshape(a.shape[1:])

    out = _layer(
        first(x), first(norm1_g), first(w_in), first(gmlp_v_g), first(gmlp_ws), first(gmlp_bs), first(gla_wa2),
        first(gla_ba), first(gla_norm_g), first(w_out), first(norm2_g), first(router_coarse_w),
        first(router_coarse_b), first(router_fine_w), first(router_fine_b), first(exp_w_gate), first(exp_w_up),
        first(exp_w_down), norm_f_g,
        tm_in=512, tb_gmlp=256, tb_gla=512, tm_out=512, tb_comb=512, sb_rows=768)
    return out.reshape(b, t, d)
```
